```python
import jax, jax.numpy as jnp
from jax import lax
import numpy as np

D_MODEL = 1024
BATCH = 1
SEQ = 16384
DEPTH = 4
DEC_BATCH = 8
DEC_SEQ = 32
PAST_LEN = 2048

CHUNK = 64
PLE_DIM = 256
N_EVEN = (DEPTH + 1) // 2
N_ODD = DEPTH // 2
EPS = 1e-6
SSD_HEAD_DIM = 64
SSD_HEADS = D_MODEL // SSD_HEAD_DIM
SSD_WIDTH = SSD_HEADS * SSD_HEAD_DIM
SSD_GROUPS = 2
SSD_STATE = 128
CONV_WIDTH = 4
CONV_CH = SSD_WIDTH + 2 * SSD_GROUPS * SSD_STATE
POOL_WINDOWS = (2, 4, 8, 16)
POOL_GROUPS = len(POOL_WINDOWS)
POOL_WIDTH = D_MODEL
POOL_GROUP_DIM = POOL_WIDTH // POOL_GROUPS
POOL_HIST = max(POOL_WINDOWS) - 1
D_IN_EVEN = SSD_WIDTH + CONV_CH + SSD_HEADS + POOL_WIDTH
QK_NOPE = 64
QK_ROPE = 32
V_DIM = 64
MLA_HEADS = D_MODEL // V_DIM
Q_LORA = 256
KV_LORA = 256
D_IN_ODD = Q_LORA + KV_LORA + QK_ROPE
ROPE_THETA = 10000.0
ATTN_Q_BLOCK = 128
PEER_HEADS = 8
PEER_NKEYS = 128
PEER_EXPERTS = PEER_NKEYS * PEER_NKEYS
PEER_QDIM = 256
PEER_HALF = PEER_QDIM // 2
PEER_TOPK = 16
PEER_TOKEN_BLOCK = 128

kernel_name = "ssd_pool_mla_peer_streaming_step"


def rmsnorm(x, w):
    xf = x.astype(jnp.float32)
    y = xf * lax.rsqrt(jnp.mean(xf * xf, axis=-1, keepdims=True) + EPS)
    return (y * w.astype(jnp.float32)).astype(x.dtype)


def rope(x, pos):
    half = QK_ROPE // 2
    inv = ROPE_THETA ** (-jnp.arange(half, dtype=jnp.float32) / half)
    ang = pos.astype(jnp.float32)[:, None] * inv[None, :]
    if x.ndim == 4:
        ang = ang[:, None, :]
    cos, sin = jnp.cos(ang), jnp.sin(ang)
    xf = x.astype(jnp.float32)
    x1, x2 = xf[..., :half], xf[..., half:]
    return jnp.concatenate([x1 * cos - x2 * sin, x1 * sin + x2 * cos], axis=-1).astype(x.dtype)


def causal_conv(x_ext, w, b):
    c = x_ext.shape[-1]
    out = lax.conv_general_dilated(x_ext, w[:, None, :].astype(x_ext.dtype), window_strides=(1,),
                                   padding='VALID', dimension_numbers=('NWC', 'WIO', 'NWC'),
                                   feature_group_count=c)
    return out + b.astype(x_ext.dtype)


def ssd_scan(x, dt, a, bm, cm, h0, block):
    f32 = jnp.float32
    b, l, g, r, p = x.shape
    n = bm.shape[-1]
    c = l // block
    x = x.astype(f32).reshape(b, c, block, g, r, p)
    dt = dt.astype(f32).reshape(b, c, block, g, r)
    bm = bm.astype(f32).reshape(b, c, block, g, n)
    cm = cm.astype(f32).reshape(b, c, block, g, n)
    da_cs = jnp.cumsum(dt * a, axis=2)
    xdt = x * dt[..., None]
    causal = jnp.tril(jnp.ones((block, block), dtype=bool))
    diff = da_cs[:, :, :, None] - da_cs[:, :, None, :]
    decay = jnp.exp(jnp.where(causal[None, None, :, :, None, None], diff, -jnp.inf))
    cb = jnp.einsum('bclgn,bcsgn->bclsg', cm, bm)
    y_diag = jnp.einsum('bclsgr,bcsgrp->bclgrp', cb[..., None] * decay, xdt)
    decay_to_end = jnp.exp(da_cs[:, :, -1:] - da_cs)
    chunk_states = jnp.einsum('bcsgn,bcsgr,bcsgrp->bcgrpn', bm, decay_to_end, xdt)
    chunk_decay = jnp.exp(da_cs[:, :, -1])

    def step(hc, inp):
        s_c, d_c = inp
        return d_c[..., None, None] * hc + s_c, hc

    h_final, h_prev = lax.scan(step, h0.astype(f32),
                               (jnp.moveaxis(chunk_states, 1, 0), jnp.moveaxis(chunk_decay, 1, 0)))
    h_prev = jnp.moveaxis(h_prev, 0, 1)
    y_off = jnp.einsum('bclgn,bcgrpn,bclgr->bclgrp', cm, h_prev, jnp.exp(da_cs))
    return (y_diag + y_off).reshape(b, l, g, r, p), h_final


def even_mixer(hn, conv_st, ssm_st, pool_st, pos0, w_in, conv_w, conv_b, dt_bias, a_log, d_skip,
               ssd_norm_w, pool_w, pool_scale, w_out):
    f32 = jnp.float32
    bsz, l, _ = hn.shape
    r = SSD_HEADS // SSD_GROUPS
    proj = hn @ w_in
    z, xbc, dt, u = jnp.split(proj, [SSD_WIDTH, SSD_WIDTH + CONV_CH, SSD_WIDTH + CONV_CH + SSD_HEADS], axis=-1)
    xbc_ext = jnp.concatenate([conv_st.astype(xbc.dtype), xbc], axis=1)
    new_conv = xbc_ext[:, -(CONV_WIDTH - 1):]
    xbc = jax.nn.silu(causal_conv(xbc_ext, conv_w, conv_b))
    xs, bm, cm = jnp.split(xbc, [SSD_WIDTH, SSD_WIDTH + SSD_GROUPS * SSD_STATE], axis=-1)
    xs = xs.reshape(bsz, l, SSD_GROUPS, r, SSD_HEAD_DIM)
    bm = bm.reshape(bsz, l, SSD_GROUPS, SSD_STATE)
    cm = cm.reshape(bsz, l, SSD_GROUPS, SSD_STATE)
    dt = jax.nn.softplus(dt.astype(f32) + dt_bias.astype(f32)).reshape(bsz, l, SSD_GROUPS, r)
    a = -jnp.exp(a_log.astype(f32)).reshape(SSD_GROUPS, r)
    h0 = ssm_st.reshape(bsz, SSD_GROUPS, r, SSD_HEAD_DIM, SSD_STATE)
    block = CHUNK if l % CHUNK == 0 else l
    y, h_final = ssd_scan(xs, dt, a, bm, cm, h0, block)
    y = y + d_skip.astype(f32).reshape(SSD_GROUPS, r)[:, :, None] * xs.astype(f32)
    y = y.reshape(bsz, l, SSD_GROUPS, r * SSD_HEAD_DIM) * jax.nn.silu(z.astype(f32)).reshape(bsz, l, SSD_GROUPS, r * SSD_HEAD_DIM)
    y = y * lax.rsqrt(jnp.mean(y * y, axis=-1, keepdims=True) + EPS)
    y = (y.reshape(bsz, l, SSD_WIDTH) * ssd_norm_w.astype(f32)).astype(hn.dtype)
    new_ssm = h_final.reshape(bsz, SSD_HEADS, SSD_HEAD_DIM, SSD_STATE)
    u_ext = jnp.concatenate([pool_st.astype(u.dtype), u], axis=1)
    new_pool = u_ext[:, -POOL_HIST:]
    cs = jnp.pad(jnp.cumsum(u_ext.astype(f32), axis=1), ((0, 0), (1, 0), (0, 0)))
    end = cs[:, POOL_HIST + 1:]
    pos = pos0 + jnp.arange(l)
    means = []
    for gi, wsz in enumerate(POOL_WINDOWS):
        sl = slice(gi * POOL_GROUP_DIM, (gi + 1) * POOL_GROUP_DIM)
        start = cs[:, POOL_HIST + 1 - wsz:POOL_HIST + 1 - wsz + l, sl]
        cnt = jnp.minimum(pos + 1, wsz).astype(f32)[None, :, None]
        means.append((end[..., sl] - start) / cnt)
    pooled = (jnp.concatenate(means, axis=-1) - u.astype(f32)).reshape(bsz, l, POOL_GROUPS, POOL_GROUP_DIM)
    yp = jnp.einsum('blgc,gcd->blgd', pooled, pool_w.astype(f32)).reshape(bsz, l, POOL_WIDTH)
    yp = (yp * pool_scale.astype(f32)).astype(hn.dtype)
    out = jnp.concatenate([y, yp], axis=-1) @ w_out
    return out, new_conv, new_ssm, new_pool


def odd_mixer(hn, ckv_hist, kpe_hist, pos0, w_in, q_norm, kv_norm, w_uq, w_ukv, w_out):
    f32 = jnp.float32
    bsz, l, _ = hn.shape
    pos = pos0 + jnp.arange(l)
    proj = hn @ w_in
    cq, ckv, kpe = jnp.split(proj, [Q_LORA, Q_LORA + KV_LORA], axis=-1)
    q = (rmsnorm(cq, q_norm) @ w_uq).reshape(bsz, l, MLA_HEADS, QK_NOPE + QK_ROPE)
    q_nope, q_pe = q[..., :QK_NOPE], rope(q[..., QK_NOPE:], pos)
    ckv = rmsnorm(ckv, kv_norm)
    kpe = rope(kpe, pos)
    ckv_all = jnp.concatenate([ckv_hist.astype(ckv.dtype), ckv], axis=1)
    kpe_all = jnp.concatenate([kpe_hist.astype(kpe.dtype), kpe], axis=1)
    n_keys = ckv_all.shape[1]
    kv = (ckv_all @ w_ukv).reshape(bsz, n_keys, MLA_HEADS, QK_NOPE + V_DIM)
    k_nope, v = kv[..., :QK_NOPE], kv[..., QK_NOPE:]
    k_chunk = jnp.arange(n_keys) // CHUNK
    scale = (QK_NOPE + QK_ROPE) ** -0.5

    def attend(args):
        qn, qp, qpos = args
        s = jnp.einsum('bqhd,bkhd->bhqk', qn, k_nope).astype(f32) + jnp.einsum('bqhr,bkr->bhqk', qp, kpe_all).astype(f32)
        mask = k_chunk[None, :] <= (qpos // CHUNK)[:, None]
        s = jnp.where(mask[None, None], s * scale, -jnp.inf)
        pr = jax.nn.softmax(s, axis=-1).astype(v.dtype)
        return jnp.einsum('bhqk,bkhd->bqhd', pr, v)

    qb = ATTN_Q_BLOCK if l % ATTN_Q_BLOCK == 0 else l
    nb = l // qb
    qn_b = q_nope.reshape(bsz, nb, qb, MLA_HEADS, QK_NOPE).transpose(1, 0, 2, 3, 4)
    qp_b = q_pe.reshape(bsz, nb, qb, MLA_HEADS, QK_ROPE).transpose(1, 0, 2, 3, 4)
    o = lax.map(attend, (qn_b, qp_b, pos.reshape(nb, qb)))
    o = o.transpose(1, 0, 2, 3, 4).reshape(bsz, l, MLA_HEADS * V_DIM)
    return o @ w_out, ckv, kpe


def peer(xn, wq, keys, u, v):
    bsz, l, d = xn.shape
    t = bsz * l
    tb = PEER_TOKEN_BLOCK if t % PEER_TOKEN_BLOCK == 0 else t

    def block_fn(xt):
        q = (xt @ wq).reshape(tb, PEER_HEADS, 2, PEER_HALF)
        s = jnp.einsum('thcd,hckd->thck', q, keys).astype(jnp.float32)
        s1, i1 = lax.top_k(s[:, :, 0], PEER_TOPK)
        s2, i2 = lax.top_k(s[:, :, 1], PEER_TOPK)
        cand_s = (s1[..., :, None] + s2[..., None, :]).reshape(tb, PEER_HEADS, PEER_TOPK * PEER_TOPK)
        cand_i = (i1[..., :, None] * PEER_NKEYS + i2[..., None, :]).reshape(tb, PEER_HEADS, PEER_TOPK * PEER_TOPK)
        top_s, sel = lax.top_k(cand_s, PEER_TOPK)
        idx = jnp.take_along_axis(cand_i, sel, axis=-1)
        g = jax.nn.softmax(top_s, axis=-1)
        u_sel = jnp.take(u, idx, axis=0)
        v_sel = jnp.take(v, idx, axis=0)
        act = jax.nn.gelu(jnp.einsum('td,thkd->thk', xt, u_sel).astype(jnp.float32))
        return jnp.einsum('thk,thkd->td', (g * act).astype(xt.dtype), v_sel)

    out = lax.map(block_fn, xn.reshape(t // tb, tb, d))
    return out.reshape(bsz, l, d)


def trunk(x, p, conv_st, ssm_st, pool_st, ckv_h, kpe_h, pos0, w):
    h = x
    convs, ssms, pools, ckvs, kpes = [], [], [], [], []
    for i in range(DEPTH):
        hn = rmsnorm(h, w['norm_mix'][i])
        if i % 2 == 0:
            e = i // 2
            out, c_new, s_new, p_new = even_mixer(
                hn, conv_st[e], ssm_st[e], pool_st[e], pos0, w['w_in_e'][e], w['conv_w'][e], w['conv_b'][e],
                w['dt_bias'][e], w['a_log'][e], w['d_skip'][e], w['ssd_norm_w'][e], w['pool_w'][e],
                w['pool_scale'][e], w['w_out_e'][e])
            convs.append(c_new)
            ssms.append(s_new)
            pools.append(p_new)
        else:
            o = i // 2
            out, ckv_new, kpe_new = odd_mixer(
                hn, ckv_h[o], kpe_h[o], pos0, w['w_in_o'][o], w['q_norm'][o], w['kv_norm'][o],
                w['w_uq'][o], w['w_ukv'][o], w['w_out_o'][o])
            ckvs.append(ckv_new)
            kpes.append(kpe_new)
        h = h + out
        h = h + peer(rmsnorm(h, w['norm_ffn'][i]), w['peer_wq'][i], w['peer_keys'][i], w['peer_u'][i], w['peer_v'][i])
        gate = jax.nn.sigmoid((rmsnorm(h, w['ple_norm'][i]) @ w['w_ple_gate'][i]).astype(jnp.float32))
        h = h + (gate * (p[i] @ w['w_ple_proj'][i]).astype(jnp.float32)).astype(h.dtype)
    y = rmsnorm(h, w['final_norm'])
    return y, jnp.stack(convs), jnp.stack(ssms), jnp.stack(pools), jnp.stack(ckvs), jnp.stack(kpes)


def setup_inputs(seed: int = 0) -> dict:
    key = jax.random.key(seed)
    ks = jax.random.split(key, 40)
    nrm = jax.random.normal
    f32 = jnp.float32
    dt0 = jnp.exp(jax.random.uniform(ks[0], (N_EVEN, SSD_HEADS), minval=np.log(1e-3), maxval=np.log(1e-1)))
    return {
        'x_prompt': nrm(ks[1], (BATCH, SEQ, D_MODEL), f32),
        'x_sample': nrm(ks[2], (DEC_BATCH, DEC_SEQ, D_MODEL), f32),
        'state_conv': nrm(ks[3], (N_EVEN, DEC_BATCH, CONV_WIDTH - 1, CONV_CH), f32),
        'state_ssm': 0.1 * nrm(ks[4], (N_EVEN, DEC_BATCH, SSD_HEADS, SSD_HEAD_DIM, SSD_STATE), f32),
        'state_pool': nrm(ks[5], (N_EVEN, DEC_BATCH, POOL_HIST, POOL_WIDTH), f32),
        'cache_ckv': nrm(ks[6], (N_ODD, DEC_BATCH, PAST_LEN, KV_LORA), f32),
        'cache_kpe': nrm(ks[7], (N_ODD, DEC_BATCH, PAST_LEN, QK_ROPE), f32),
        'p_prompt': nrm(ks[8], (DEPTH, BATCH, SEQ, PLE_DIM), f32),
        'p_sample': nrm(ks[9], (DEPTH, DEC_BATCH, DEC_SEQ, PLE_DIM), f32),
        'norm_mix': 1.0 + 0.02 * nrm(ks[10], (DEPTH, D_MODEL), f32),
        'norm_ffn': 1.0 + 0.02 * nrm(ks[11], (DEPTH, D_MODEL), f32),
        'ple_norm': 1.0 + 0.02 * nrm(ks[12], (DEPTH, D_MODEL), f32),
        'final_norm': 1.0 + 0.02 * nrm(ks[13], (D_MODEL,), f32),
        'w_in_e': nrm(ks[14], (N_EVEN, D_MODEL, D_IN_EVEN), f32) * D_MODEL ** -0.5,
        'conv_w': nrm(ks[15], (N_EVEN, CONV_WIDTH, CONV_CH), f32) * CONV_WIDTH ** -0.5,
        'conv_b': 0.02 * nrm(ks[16], (N_EVEN, CONV_CH), f32),
        'dt_bias': dt0 + jnp.log(-jnp.expm1(-dt0)),
        'a_log': jnp.log(jax.random.uniform(ks[17], (N_EVEN, SSD_HEADS), minval=1.0, maxval=16.0)),
        'd_skip': 1.0 + 0.02 * nrm(ks[18], (N_EVEN, SSD_HEADS), f32),
        'ssd_norm_w': 1.0 + 0.02 * nrm(ks[19], (N_EVEN, SSD_WIDTH), f32),
        'pool_w': nrm(ks[20], (N_EVEN, POOL_GROUPS, POOL_GROUP_DIM, POOL_GROUP_DIM), f32) * POOL_GROUP_DIM ** -0.5,
        'pool_scale': 1.0 + 0.05 * nrm(ks[21], (N_EVEN, POOL_WIDTH), f32),
        'w_out_e': nrm(ks[22], (N_EVEN, SSD_WIDTH + POOL_WIDTH, D_MODEL), f32) * (SSD_WIDTH + POOL_WIDTH) ** -0.5,
        'w_in_o': nrm(ks[23], (N_ODD, D_MODEL, D_IN_ODD), f32) * D_MODEL ** -0.5,
        'q_norm': 1.0 + 0.02 * nrm(ks[24], (N_ODD, Q_LORA), f32),
        'kv_norm': 1.0 + 0.02 * nrm(ks[25], (N_ODD, KV_LORA), f32),
        'w_uq': nrm(ks[26], (N_ODD, Q_LORA, MLA_HEADS * (QK_NOPE + QK_ROPE)), f32) * Q_LORA ** -0.5,
        'w_ukv': nrm(ks[27], (N_ODD, KV_LORA, MLA_HEADS * (QK_NOPE + V_DIM)), f32) * KV_LORA ** -0.5,
        'w_out_o': nrm(ks[28], (N_ODD, MLA_HEADS * V_DIM, D_MODEL), f32) * (MLA_HEADS * V_DIM) ** -0.5,
        'peer_wq': nrm(ks[29], (DEPTH, D_MODEL, PEER_HEADS * PEER_QDIM), f32) * D_MODEL ** -0.5,
        'peer_keys': nrm(ks[30], (DEPTH, PEER_HEADS, 2, PEER_NKEYS, PEER_HALF), f32) * PEER_HALF ** -0.5,
        'peer_u': nrm(ks[31], (DEPTH, PEER_EXPERTS, D_MODEL), f32) * D_MODEL ** -0.5,
        'peer_v': 0.1 * nrm(ks[32], (DEPTH, PEER_EXPERTS, D_MODEL), f32),
        'w_ple_proj': nrm(ks[33], (DEPTH, PLE_DIM, D_MODEL), f32) * PLE_DIM ** -0.5,
        'w_ple_gate': nrm(ks[34], (DEPTH, D_MODEL, D_MODEL), f32) * D_MODEL ** -0.5,
    }


def reference(x_prompt, x_sample, state_conv, state_ssm, state_pool, cache_ckv, cache_kpe, p_prompt, p_sample,
              norm_mix, norm_ffn, ple_norm, final_norm, w_in_e, conv_w, conv_b, dt_bias, a_log, d_skip,
              ssd_norm_w, pool_w, pool_scale, w_out_e, w_in_o, q_norm, kv_norm, w_uq, w_ukv, w_out_o,
              peer_wq, peer_keys, peer_u, peer_v, w_ple_proj, w_ple_gate):
    w = dict(norm_mix=norm_mix, norm_ffn=norm_ffn, ple_norm=ple_norm, final_norm=final_norm,
             w_in_e=w_in_e, conv_w=conv_w, conv_b=conv_b, dt_bias=dt_bias, a_log=a_log, d_skip=d_skip,
             ssd_norm_w=ssd_norm_w, pool_w=pool_w, pool_scale=pool_scale, w_out_e=w_out_e,
             w_in_o=w_in_o, q_norm=q_norm, kv_norm=kv_norm, w_uq=w_uq, w_ukv=w_ukv, w_out_o=w_out_o,
             peer_wq=peer_wq, peer_keys=peer_keys, peer_u=peer_u, peer_v=peer_v,
             w_ple_proj=w_ple_proj, w_ple_gate=w_ple_gate)
    b0 = x_prompt.shape[0]
    dtp = x_prompt.dtype
    conv0 = jnp.zeros((N_EVEN, b0, CONV_WIDTH - 1, CONV_CH), dtp)
    ssm0 = jnp.zeros((N_EVEN, b0, SSD_HEADS, SSD_HEAD_DIM, SSD_STATE), jnp.float32)
    pool0 = jnp.zeros((N_EVEN, b0, POOL_HIST, POOL_WIDTH), dtp)
    ckv0 = jnp.zeros((N_ODD, b0, 0, KV_LORA), dtp)
    kpe0 = jnp.zeros((N_ODD, b0, 0, QK_ROPE), dtp)
    y_prompt, conv_p, ssm_p, pool_p, ckv_p, kpe_p = trunk(x_prompt, p_prompt, conv0, ssm0, pool0, ckv0, kpe0, 0, w)
    pos0 = cache_ckv.shape[2]
    y_sample, conv_s, ssm_s, pool_s, ckv_s, kpe_s = trunk(x_sample, p_sample, state_conv, state_ssm, state_pool,
                                                          cache_ckv, cache_kpe, pos0, w)
    return (y_prompt, y_sample, conv_p, ssm_p, pool_p, ckv_p, kpe_p, conv_s, ssm_s, pool_s, ckv_s, kpe_s)
```

```python
import functools

import numpy as np
import jax
import jax.numpy as jnp
from jax import lax
from jax.experimental import pallas as pl
from jax.experimental.pallas import tpu as pltpu

F32 = jnp.float32
BF16 = jnp.bfloat16
I32 = jnp.int32

D_MODEL = 1024
DEPTH = 4
CHUNK = 64
EPS = 1e-6
SSD_HEAD_DIM = 64
SSD_HEADS = 16
SSD_WIDTH = 1024
SSD_GROUPS = 2
SSD_STATE = 128
CONV_WIDTH = 4
CONV_CH = 1536
POOL_WINDOWS = (2, 4, 8, 16)
POOL_GROUP_DIM = 256
POOL_HIST = 15
QK_NOPE = 64
QK_ROPE = 32
V_DIM = 64
MLA_HEADS = 16
Q_LORA = 256
KV_LORA = 256
ROPE_THETA = 10000.0
PEER_HEADS = 8
PEER_NKEYS = 128
PEER_HALF = 128
PEER_TOPK = 16

LANES = 128
VMEM_LIMIT_BYTES = 56 * 1024 * 1024
SSD_BLOCK = 128
HEAD_SLOT = 128

NEG_INF = float("-inf")


def _cparams(*sem):
    return pltpu.CompilerParams(dimension_semantics=sem, vmem_limit_bytes=VMEM_LIMIT_BYTES)


def _dot(a, b):
    return jnp.dot(a, b, preferred_element_type=F32)


def _dot_nt(a, b):
    return lax.dot_general(a, b, (((1,), (1,)), ((), ())), preferred_element_type=F32)


def _split3(x):
    hi = x.astype(BF16)
    r = x - hi.astype(F32)
    mid = r.astype(BF16)
    lo = (r - mid.astype(F32)).astype(BF16)
    return hi, mid, lo


def _sel_rhs(x, sel):
    hi, mid, lo = _split3(x)
    return _dot(hi, sel) + _dot(mid, sel) + _dot(lo, sel)


def _sel_lhs(sel, x):
    hi, mid, lo = _split3(x)
    return _dot(sel, hi) + _dot(sel, mid) + _dot(sel, lo)


def _rms(x, w):
    return x * lax.rsqrt(jnp.mean(x * x, axis=-1, keepdims=True) + EPS) * w


def _silu(x):
    return x * (1.0 / (1.0 + jnp.exp(-x)))


def _softplus(x):
    return jnp.maximum(x, 0.0) + jnp.log(1.0 + jnp.exp(-jnp.abs(x)))


def _full(shape):
    nd = len(shape)
    return pl.BlockSpec(shape, lambda *_: (0,) * nd)


EVEN_COLS = SSD_WIDTH + CONV_CH + SSD_WIDTH + LANES


def _even_in_kernel(h_ref, nw_ref, w_ref, wdt_ref, z_ref, xbc_ref, u_ref, dt_ref, dtt_ref):
    xn = _rms(h_ref[...], nw_ref[...]).astype(BF16)
    proj = _dot(xn, w_ref[...])
    z_ref[...] = proj[:, :SSD_WIDTH]
    xbc_ref[...] = proj[:, SSD_WIDTH:SSD_WIDTH + CONV_CH]
    u_ref[...] = proj[:, SSD_WIDTH + CONV_CH:2 * SSD_WIDTH + CONV_CH]
    dt_ref[...] = proj[:, 2 * SSD_WIDTH + CONV_CH:]
    dtt_ref[...] = _dot_nt(wdt_ref[...], xn)


def _even_in(h, nw, w, wdt, tm):
    t = h.shape[0]
    row = lambda n: pl.BlockSpec((tm, n), lambda i: (i, 0))
    return pl.pallas_call(
        _even_in_kernel,
        grid=(t // tm,),
        in_specs=[row(D_MODEL), _full((1, D_MODEL)), _full((D_MODEL, EVEN_COLS)), _full((SSD_HEADS, D_MODEL))],
        out_specs=[row(SSD_WIDTH), row(CONV_CH), row(SSD_WIDTH), row(LANES),
                   pl.BlockSpec((SSD_HEADS, tm), lambda i: (0, i))],
        out_shape=[jax.ShapeDtypeStruct((t, SSD_WIDTH), F32), jax.ShapeDtypeStruct((t, CONV_CH), F32),
                   jax.ShapeDtypeStruct((t, SSD_WIDTH), F32), jax.ShapeDtypeStruct((t, LANES), F32),
                   jax.ShapeDtypeStruct((SSD_HEADS, t), F32)],
        compiler_params=_cparams("parallel"),
        name="even_in",
    )(h, nw, w, wdt)


def _ssd_kernel(xbc_ref, z_ref, dt_ref, dtt_ref, hist_ref, s0_ref, cw_ref, cb_ref, bias_ref, biast_ref,
                a_ref, at_ref, dskip_ref, nw_ref, expand_ref, y_ref, sfin_ref, ext_ref, st_ref, *, valid_len):
    q = SSD_BLOCK
    c = pl.program_id(1)

    @pl.when(c == 0)
    def _():
        ext_ref[0:8, :] = hist_ref[0]
        st_ref[...] = s0_ref[0]

    ext_ref[8:8 + q, :] = xbc_ref[0]
    conv = cb_ref[...] + cw_ref[0:1, :] * ext_ref[5:5 + q, :]
    for j in range(1, CONV_WIDTH):
        conv = conv + cw_ref[j:j + 1, :] * ext_ref[5 + j:5 + j + q, :]
    ext_ref[0:8, :] = ext_ref[q:q + 8, :]
    xbc = _silu(conv)
    xs = xbc[:, :SSD_WIDTH]
    xs_b = xs.astype(BF16)

    row_i = lax.broadcasted_iota(I32, (q, q), 0)
    col_i = lax.broadcasted_iota(I32, (q, q), 1)
    tril = (row_i >= col_i).astype(BF16)
    triu = (row_i <= col_i).astype(BF16)
    ones = jnp.ones((q, q), BF16)
    tok_ok = (c * q + lax.broadcasted_iota(I32, (q, LANES), 0)) < valid_len
    dt_tok = jnp.where(tok_ok, _softplus(dt_ref[0] + bias_ref[...]), 0.0)
    cs_tok = _sel_lhs(tril, dt_tok * -jnp.exp(a_ref[...]))
    lane_ok = (c * q + lax.broadcasted_iota(I32, (SSD_HEADS, q), 1)) < valid_len
    dt_t = jnp.where(lane_ok, _softplus(dtt_ref[0] + biast_ref[...]), 0.0)
    da_t = dt_t * -jnp.exp(at_ref[...])
    cs_t = _sel_rhs(da_t, triu)
    tot_t = _sel_rhs(da_t, ones)
    w_t = dt_t * jnp.exp(tot_t - cs_t)
    cs_exp = _sel_rhs(cs_tok, expand_ref[...])
    ecs = jnp.exp(cs_exp)
    dchunk = ecs[q - 1:q, :]

    lane_lo = lax.broadcasted_iota(I32, (q, LANES), 1) < SSD_HEAD_DIM
    causal = row_i >= col_i
    r = SSD_HEADS // SSD_GROUPS
    gw = r * SSD_HEAD_DIM
    y_parts = []
    for g in range(SSD_GROUPS):
        bm = xbc[:, SSD_WIDTH + g * SSD_STATE:SSD_WIDTH + (g + 1) * SSD_STATE]
        cm = xbc[:, SSD_WIDTH + (SSD_GROUPS + g) * SSD_STATE:SSD_WIDTH + (SSD_GROUPS + g + 1) * SSD_STATE]
        cm_b = cm.astype(BF16)
        cb = _dot_nt(cm_b, bm.astype(BF16))
        bm_t = bm.T
        st_g = st_ref[g]
        y_off = _dot(cm_b, st_g.astype(BF16)) * ecs[:, g * gw:(g + 1) * gw]
        yd, up = [], []
        for pr in range(r // 2):
            res_y, res_u = [], []
            xs_pair = xs_b[:, g * gw + pr * LANES:g * gw + (pr + 1) * LANES]
            for k in range(2):
                hh = g * r + 2 * pr + k
                seg = cs_tok[:, hh:hh + 1] - cs_t[hh:hh + 1, :]
                lmat = jnp.where(causal, jnp.exp(seg), 0.0)
                m = (cb * lmat * dt_t[hh:hh + 1, :]).astype(BF16)
                res_y.append(_dot(m, xs_pair))
                res_u.append(_dot((bm_t * w_t[hh:hh + 1, :]).astype(BF16), xs_pair))
            yd.append(jnp.where(lane_lo, res_y[0], res_y[1]))
            up.append(jnp.where(lane_lo, res_u[0], res_u[1]))
        y_parts.append(jnp.concatenate(yd, axis=1) + y_off)
        st_ref[g] = dchunk[:, g * gw:(g + 1) * gw] * st_g + jnp.concatenate(up, axis=1)

    y = jnp.concatenate(y_parts, axis=1) + dskip_ref[...] * xs
    y = y * _silu(z_ref[0])
    outs = []
    for g in range(SSD_GROUPS):
        yg = y[:, g * gw:(g + 1) * gw]
        outs.append(yg * lax.rsqrt(jnp.mean(yg * yg, axis=-1, keepdims=True) + EPS))
    y_ref[0] = (jnp.concatenate(outs, axis=1) * nw_ref[...]).astype(y_ref.dtype)

    @pl.when(c == pl.num_programs(1) - 1)
    def _():
        sfin_ref[0] = st_ref[...]


def _ssd(xbc, z, dt, dtt, hist, s0, p, valid_len):
    b, l, _ = xbc.shape
    q = SSD_BLOCK
    n_state = SSD_GROUPS * SSD_STATE
    gw = SSD_WIDTH // SSD_GROUPS
    blk = lambda n: pl.BlockSpec((1, q, n), lambda i, c: (i, c, 0))
    per_b = lambda s: pl.BlockSpec((1,) + s, lambda i, c: (i,) + (0,) * len(s))
    return pl.pallas_call(
        functools.partial(_ssd_kernel, valid_len=valid_len),
        grid=(b, l // q),
        in_specs=[blk(CONV_CH), blk(SSD_WIDTH), blk(LANES),
                  pl.BlockSpec((1, SSD_HEADS, q), lambda i, c: (i, 0, c)),
                  per_b((8, CONV_CH)), per_b((SSD_GROUPS, SSD_STATE, gw)),
                  _full((CONV_WIDTH, CONV_CH)), _full((1, CONV_CH)), _full((1, LANES)), _full((SSD_HEADS, q)),
                  _full((1, LANES)), _full((SSD_HEADS, q)), _full((1, SSD_WIDTH)), _full((1, SSD_WIDTH)),
                  _full((LANES, SSD_WIDTH))],
        out_specs=[blk(SSD_WIDTH), per_b((SSD_GROUPS, SSD_STATE, gw))],
        out_shape=[jax.ShapeDtypeStruct((b, l, SSD_WIDTH), BF16),
                   jax.ShapeDtypeStruct((b, SSD_GROUPS, SSD_STATE, gw), F32)],
        scratch_shapes=[pltpu.VMEM((q + 8, CONV_CH), F32), pltpu.VMEM((SSD_GROUPS, SSD_STATE, gw), F32)],
        compiler_params=_cparams("parallel", "arbitrary"),
        name="ssd_scan",
    )(xbc, z, dt, dtt, hist, s0, p["conv_w"], p["conv_b"], p["dt_bias"], p["dt_bias_t"], p["a_log"], p["a_log_t"],
      p["d_skip"], p["ssd_norm_w"], p["expand"])


def _pool_out_kernel(u_ref, hist_ref, y_ref, h_ref, pw_ref, ps_ref, wo_ref, o_ref, ext_ref, *, pos0, tm):
    i = pl.program_id(1)

    @pl.when(i == 0)
    def _():
        ext_ref[0:16, :] = hist_ref[0]

    ext_ref[16:16 + tm, :] = u_ref[0]
    pos = pos0 + i * tm + lax.broadcasted_iota(I32, (tm, POOL_GROUP_DIM), 0)
    acc = h_ref[0] + _dot(y_ref[0], wo_ref[0:SSD_WIDTH, :])
    yps = []
    for gi, wsz in enumerate(POOL_WINDOWS):
        c0 = gi * POOL_GROUP_DIM
        cur = ext_ref[16:16 + tm, c0:c0 + POOL_GROUP_DIM]
        tot = cur
        for j in range(1, wsz):
            tot = tot + ext_ref[16 - j:16 - j + tm, c0:c0 + POOL_GROUP_DIM]
        cnt = jnp.minimum(pos + 1, wsz).astype(F32)
        pooled = tot / cnt - cur
        yps.append(_dot(pooled.astype(BF16), pw_ref[gi]))
    yp = (jnp.concatenate(yps, axis=1) * ps_ref[...]).astype(BF16)
    o_ref[0] = acc + _dot(yp, wo_ref[SSD_WIDTH:, :])
    ext_ref[0:16, :] = ext_ref[tm:tm + 16, :]


def _pool_out(u, hist, y, h, p, pos0, tm):
    b, l, _ = u.shape
    blk = lambda n: pl.BlockSpec((1, tm, n), lambda bi, i: (bi, i, 0))
    return pl.pallas_call(
        functools.partial(_pool_out_kernel, pos0=pos0, tm=tm),
        grid=(b, l // tm),
        in_specs=[blk(SSD_WIDTH), pl.BlockSpec((1, 16, SSD_WIDTH), lambda bi, i: (bi, 0, 0)), blk(SSD_WIDTH),
                  blk(D_MODEL), _full((len(POOL_WINDOWS), POOL_GROUP_DIM, POOL_GROUP_DIM)), _full((1, SSD_WIDTH)),
                  _full((2 * SSD_WIDTH, D_MODEL))],
        out_specs=blk(D_MODEL),
        out_shape=jax.ShapeDtypeStruct((b, l, D_MODEL), F32),
        scratch_shapes=[pltpu.VMEM((tm + 16, SSD_WIDTH), F32)],
        compiler_params=_cparams("parallel", "arbitrary"),
        name="pool_out",
    )(u, hist, y, h, p["pool_w"], p["pool_scale"], p["w_out"])


ODD_COLS = Q_LORA + KV_LORA + 2 * LANES
QK_ROWS = MLA_HEADS * HEAD_SLOT


def _odd_in_kernel(h_ref, nw_ref, w_ref, qn_ref, kvn_ref, wa_ref, wb_ref, cos_ref, sin_ref, r128_ref, rq_ref,
                   rqm_ref, ckv_ref, kpe_ref, qt_ref, *, scale):
    xn = _rms(h_ref[...], nw_ref[...]).astype(BF16)
    proj = _dot(xn, w_ref[...])
    cq = _rms(proj[:, :Q_LORA], qn_ref[...]).astype(BF16)
    ckv_ref[...] = _rms(proj[:, Q_LORA:Q_LORA + KV_LORA], kvn_ref[...])
    cos = cos_ref[...]
    sin = sin_ref[...]
    kpe = proj[:, Q_LORA + KV_LORA:Q_LORA + KV_LORA + LANES]
    kpe_sw = proj[:, Q_LORA + KV_LORA + LANES:]
    kpe_rot = kpe * _sel_rhs(cos, r128_ref[...]) + kpe_sw * _sel_rhs(sin, r128_ref[...])
    kpe_ref[...] = kpe_rot[:, :QK_ROPE]
    cos_q = (_sel_rhs(cos, rq_ref[...]) + rqm_ref[...]) * scale
    sin_q = _sel_rhs(sin, rq_ref[...]) * scale
    q = _dot(cq, wa_ref[...]) * cos_q + _dot(cq, wb_ref[...]) * sin_q
    qt_ref[...] = q.T.astype(qt_ref.dtype)


def _odd_in(h, cos, sin, p, tm):
    t = h.shape[0]
    row = lambda n: pl.BlockSpec((tm, n), lambda i: (i, 0))
    half = QK_ROPE // 2
    scale = float((QK_NOPE + QK_ROPE) ** -0.5)
    return pl.pallas_call(
        functools.partial(_odd_in_kernel, scale=scale),
        grid=(t // tm,),
        in_specs=[row(D_MODEL), _full((1, D_MODEL)), _full((D_MODEL, ODD_COLS)), _full((1, Q_LORA)),
                  _full((1, KV_LORA)), _full((Q_LORA, QK_ROWS)), _full((Q_LORA, QK_ROWS)), row(half), row(half),
                  _full((half, LANES)), _full((half, QK_ROWS)), _full((1, QK_ROWS))],
        out_specs=[row(KV_LORA), row(QK_ROPE), pl.BlockSpec((QK_ROWS, tm), lambda i: (0, i))],
        out_shape=[jax.ShapeDtypeStruct((t, KV_LORA), F32), jax.ShapeDtypeStruct((t, QK_ROPE), F32),
                   jax.ShapeDtypeStruct((QK_ROWS, t), BF16)],
        compiler_params=_cparams("parallel"),
        name="odd_in",
    )(h, p["norm_mix"], p["w_in"], p["q_norm"], p["kv_norm"], p["w_qa"], p["w_qb"], cos, sin, p["r128"], p["rq"],
      p["rq_mask"])


def _kv_expand_kernel(ckv_ref, kpe_ref, wk_ref, rk_ref, wvt_ref, k_ref, vt_ref):
    ckv = ckv_ref[0].astype(BF16)
    k_ref[0] = (_dot(ckv, wk_ref[...]) + _dot(kpe_ref[0].astype(BF16), rk_ref[...])).astype(k_ref.dtype)
    vt_ref[0] = _dot_nt(wvt_ref[...], ckv).astype(vt_ref.dtype)


def _kv_expand(ckv, kpe, p, tm):
    b, t, _ = ckv.shape
    vrows = MLA_HEADS * V_DIM
    return pl.pallas_call(
        _kv_expand_kernel,
        grid=(b, t // tm),
        in_specs=[pl.BlockSpec((1, tm, KV_LORA), lambda bi, i: (bi, i, 0)),
                  pl.BlockSpec((1, tm, QK_ROPE), lambda bi, i: (bi, i, 0)),
                  _full((KV_LORA, QK_ROWS)), _full((QK_ROPE, QK_ROWS)), _full((vrows, KV_LORA))],
        out_specs=[pl.BlockSpec((1, tm, QK_ROWS), lambda bi, i: (bi, i, 0)),
                   pl.BlockSpec((1, vrows, tm), lambda bi, i: (bi, 0, i))],
        out_shape=[jax.ShapeDtypeStruct((b, t, QK_ROWS), BF16), jax.ShapeDtypeStruct((b, vrows, t), BF16)],
        compiler_params=_cparams("parallel", "parallel"),
        name="kv_expand",
    )(ckv, kpe, p["w_k"], p["r_k"], p["w_vt"])


def _attn_kernel(qi_ref, ki_ref, qt_ref, k_ref, vt_ref, o_ref, m_ref, l_ref, acc_ref, *, tq, tk, causal,
                 kv_valid):
    s_idx = pl.program_id(2)
    qi = qi_ref[s_idx]
    ki = ki_ref[s_idx]

    @pl.when(ki == 0)
    def _():
        m_ref[...] = jnp.full(m_ref.shape, NEG_INF, F32)
        l_ref[...] = jnp.zeros(l_ref.shape, F32)
        acc_ref[...] = jnp.zeros(acc_ref.shape, F32)

    def step(masked):
        st = _dot(k_ref[0], qt_ref[0])
        if masked:
            kpos = ki * tk + lax.broadcasted_iota(I32, (tk, tq), 0)
            if causal:
                qpos = qi * tq + lax.broadcasted_iota(I32, (tk, tq), 1)
                ok = (kpos // CHUNK) <= (qpos // CHUNK)
            else:
                ok = kpos < kv_valid
            st = jnp.where(ok, st, NEG_INF)
        m_prev = m_ref[...]
        m_new = jnp.maximum(m_prev, jnp.max(st, axis=0, keepdims=True))
        alpha = jnp.exp(m_prev - m_new)
        pt = jnp.exp(st - m_new)
        l_ref[...] = alpha * l_ref[...] + jnp.sum(pt, axis=0, keepdims=True)
        acc_ref[...] = alpha * acc_ref[...] + _dot(vt_ref[0], pt.astype(BF16))
        m_ref[...] = m_new

    if causal:
        last = ((qi + 1) * tq - 1) // tk

        @pl.when(ki < last)
        def _():
            step(False)

        @pl.when(ki == last)
        def _():
            step(True)
            o_ref[0] = (acc_ref[...] / l_ref[...]).astype(o_ref.dtype)
    else:
        step(kv_valid is not None)
        o_ref[0] = (acc_ref[...] / l_ref[...]).astype(o_ref.dtype)


def _attention(qt, k, vt, tq, tk, causal, kv_valid):
    b, _, t_q = qt.shape
    t_k = k.shape[1]
    nq, nk = t_q // tq, t_k // tk
    if causal:
        pairs = [(a, c) for a in range(nq) for c in range(((a + 1) * tq - 1) // tk + 1)]
    else:
        pairs = [(a, c) for a in range(nq) for c in range(nk)]
        assert nk == 1
    qi = jnp.asarray(np.array([a for a, _ in pairs], np.int32))
    ki = jnp.asarray(np.array([c for _, c in pairs], np.int32))
    grid_spec = pltpu.PrefetchScalarGridSpec(
        num_scalar_prefetch=2,
        grid=(b, MLA_HEADS, len(pairs)),
        in_specs=[pl.BlockSpec((1, HEAD_SLOT, tq), lambda bi, hi, s, qi_r, ki_r: (bi, hi, qi_r[s])),
                  pl.BlockSpec((1, tk, HEAD_SLOT), lambda bi, hi, s, qi_r, ki_r: (bi, ki_r[s], hi)),
                  pl.BlockSpec((1, V_DIM, tk), lambda bi, hi, s, qi_r, ki_r: (bi, hi, ki_r[s]))],
        out_specs=pl.BlockSpec((1, V_DIM, tq), lambda bi, hi, s, qi_r, ki_r: (bi, hi, qi_r[s])),
        scratch_shapes=[pltpu.VMEM((1, tq), F32), pltpu.VMEM((1, tq), F32), pltpu.VMEM((V_DIM, tq), F32)],
    )
    return pl.pallas_call(
        functools.partial(_attn_kernel, tq=tq, tk=tk, causal=causal, kv_valid=kv_valid),
        grid_spec=grid_spec,
        out_shape=jax.ShapeDtypeStruct((b, MLA_HEADS * V_DIM, t_q), F32),
        compiler_params=_cparams("parallel", "parallel", "arbitrary"),
        name="attention",
    )(qi, ki, qt, k, vt)


def _attn_out_kernel(ot_ref, h_ref, wo_ref, o_ref):
    o_ref[...] = h_ref[...] + _dot(ot_ref[...].T.astype(BF16), wo_ref[...])


def _attn_out(ot, h, wo, tm):
    t = h.shape[0]
    return pl.pallas_call(
        _attn_out_kernel,
        grid=(t // tm,),
        in_specs=[pl.BlockSpec((MLA_HEADS * V_DIM, tm), lambda i: (0, i)),
                  pl.BlockSpec((tm, D_MODEL), lambda i: (i, 0)), _full((MLA_HEADS * V_DIM, D_MODEL))],
        out_specs=pl.BlockSpec((tm, D_MODEL), lambda i: (i, 0)),
        out_shape=jax.ShapeDtypeStruct((t, D_MODEL), F32),
        compiler_params=_cparams("parallel"),
        name="attn_out",
    )(ot, h, wo)


_STAIR = [(a, b) for a in range(PEER_TOPK) for b in range(PEER_TOPK) if (a + 1) * (b + 1) <= PEER_TOPK]
STAIR_ROWS = 56
NOT_RANKED = PEER_TOPK
PAD_FLAT_INDEX = PEER_TOPK * PEER_TOPK


def _topk_columns(s, key_iota):
    rank = jnp.full(s.shape, NOT_RANKED, I32)
    vals = []
    x = s
    for a in range(PEER_TOPK):
        m = jnp.max(x, axis=0, keepdims=True)
        first = jnp.min(jnp.where(x == m, key_iota, PEER_NKEYS), axis=0, keepdims=True)
        hit = key_iota == first
        rank = jnp.where(hit, a, rank)
        x = jnp.where(hit, NEG_INF, x)
        vals.append(m)
    return jnp.concatenate(vals, axis=0), rank


def _peer_route_kernel(h_ref, nw_ref, wqh_ref, wql_ref, kh_ref, kl_ref, sela_ref, selb_ref, fidx_ref,
                       xn_ref, e1_ref, mb_ref, e2_ref, bb_ref):
    xn = _rms(h_ref[...], nw_ref[...])
    xn_hi = xn.astype(BF16)
    xn_lo = (xn - xn_hi.astype(F32)).astype(BF16)
    xn_ref[...] = xn_hi
    qt = _dot_nt(wqh_ref[...], xn_hi) + _dot_nt(wqh_ref[...], xn_lo) + _dot_nt(wql_ref[...], xn_hi)
    tm = qt.shape[1]
    key_iota = lax.broadcasted_iota(I32, (PEER_NKEYS, tm), 0)
    a_iota = lax.broadcasted_iota(I32, (PEER_TOPK, tm), 0)
    fidx = fidx_ref[...]
    for hd in range(PEER_HEADS):
        vals, ranks, scores = [], [], []
        for c in range(2):
            hc = 2 * hd + c
            qs = qt[hc * PEER_HALF:(hc + 1) * PEER_HALF, :]
            q_hi = qs.astype(BF16)
            q_lo = (qs - q_hi.astype(F32)).astype(BF16)
            s = _dot(kh_ref[hc], q_hi) + _dot(kh_ref[hc], q_lo) + _dot(kl_ref[hc], q_hi)
            v, rk = _topk_columns(s, key_iota)
            vals.append(v)
            ranks.append(rk)
            scores.append(s)
        cand = _sel_lhs(sela_ref[...], vals[0]) + _sel_lhs(selb_ref[...], vals[1])
        cand = jnp.where(fidx < PAD_FLAT_INDEX, cand, NEG_INF)
        top = None
        zsum = None
        mb = jnp.zeros((PEER_TOPK, tm), I32)
        for k in range(PEER_TOPK):
            m = jnp.max(cand, axis=0, keepdims=True)
            f = jnp.min(jnp.where(cand == m, fidx, PAD_FLAT_INDEX), axis=0, keepdims=True)
            cand = jnp.where(fidx == f, NEG_INF, cand)
            if k == 0:
                top = m
                zsum = jnp.ones_like(m)
            else:
                zsum = zsum + jnp.exp(m - top)
            bit = lax.shift_left(jnp.ones_like(f), f & (PEER_TOPK - 1))
            mb = jnp.where(a_iota == (f >> 4), mb | bit, mb)
        mbsel = jnp.zeros((PEER_NKEYS, tm), I32)
        for a in range(PEER_TOPK):
            mbsel = jnp.where(ranks[0] == a, mb[a:a + 1, :], mbsel)
        inv_z = 1.0 / zsum
        e1 = jnp.where(ranks[0] < NOT_RANKED, jnp.exp(scores[0] - vals[0][0:1, :]), 0.0) * inv_z
        e2 = jnp.where(ranks[1] < NOT_RANKED, jnp.exp(scores[1] - vals[1][0:1, :]), 0.0)
        bitb = jnp.where(ranks[1] < NOT_RANKED, lax.shift_left(jnp.ones_like(ranks[1]), ranks[1] & 15), 0)
        e1_ref[hd] = e1
        mb_ref[hd] = mbsel
        e2_ref[hd] = e2
        bb_ref[hd] = bitb


def _peer_route(h, p, tm):
    t = h.shape[0]
    tab = pl.BlockSpec((PEER_HEADS, PEER_NKEYS, tm), lambda i: (0, 0, i))
    tab_shape = lambda dt: jax.ShapeDtypeStruct((PEER_HEADS, PEER_NKEYS, t), dt)
    fidx = jnp.asarray(np.tile(np.array([a * PEER_TOPK + b for a, b in _STAIR]
                                        + [PAD_FLAT_INDEX] * (STAIR_ROWS - len(_STAIR)), np.int32)[:, None], (1, tm)))
    return pl.pallas_call(
        _peer_route_kernel,
        grid=(t // tm,),
        in_specs=[pl.BlockSpec((tm, D_MODEL), lambda i: (i, 0)), _full((1, D_MODEL)),
                  _full((2 * PEER_HEADS * PEER_HALF, D_MODEL)), _full((2 * PEER_HEADS * PEER_HALF, D_MODEL)),
                  _full((2 * PEER_HEADS, PEER_NKEYS, PEER_HALF)), _full((2 * PEER_HEADS, PEER_NKEYS, PEER_HALF)),
                  _full((STAIR_ROWS, PEER_TOPK)), _full((STAIR_ROWS, PEER_TOPK)), _full((STAIR_ROWS, tm))],
        out_specs=[pl.BlockSpec((tm, D_MODEL), lambda i: (i, 0)), tab, tab, tab, tab],
        out_shape=[jax.ShapeDtypeStruct((t, D_MODEL), BF16), tab_shape(F32), tab_shape(I32), tab_shape(F32),
                   tab_shape(I32)],
        compiler_params=_cparams("parallel"),
        name="peer_route",
    )(h, p["norm_ffn"], p["wq_hi"], p["wq_lo"], p["keys_hi"], p["keys_lo"], p["sel_a"], p["sel_b"], fidx)


EXPERT_BLOCK = 1024
ROWS_PER_BLOCK = EXPERT_BLOCK // PEER_NKEYS


def _gelu_tanh(x):
    return 0.5 * x * (1.0 + jnp.tanh(0.7978845608028654 * (x + 0.044715 * (x * x * x))))


def _peer_mix_kernel(h_ref, xn_ref, e1_ref, mb_ref, e2_ref, bb_ref, u_ref, vt_ref, o_ref, wg_ref, acc_ref):
    j = pl.program_id(1)

    @pl.when(j == 0)
    def _():
        acc_ref[...] = jnp.zeros(acc_ref.shape, F32)

    ht = _dot_nt(u_ref[...], xn_ref[...])
    for r in range(ROWS_PER_BLOCK):
        w = None
        for hd in range(PEER_HEADS):
            hit = (mb_ref[hd, r:r + 1, :] & bb_ref[hd]) != 0
            term = jnp.where(hit, e1_ref[hd, r:r + 1, :] * e2_ref[hd], 0.0)
            w = term if w is None else w + term
        rows = slice(r * PEER_NKEYS, (r + 1) * PEER_NKEYS)
        wg_ref[rows, :] = (w * _gelu_tanh(ht[rows, :])).astype(BF16)
    acc_ref[...] += _dot(vt_ref[...], wg_ref[...])

    @pl.when(j == pl.num_programs(1) - 1)
    def _():
        o_ref[...] = h_ref[...] + acc_ref[...].T


def _peer_mix(h, xn, e1, mb, e2, bb, u, vt, tm):
    t = h.shape[0]
    n_exp = u.shape[0]
    rowblk = pl.BlockSpec((PEER_HEADS, ROWS_PER_BLOCK, tm), lambda i, j: (0, j, i))
    fullblk = pl.BlockSpec((PEER_HEADS, PEER_NKEYS, tm), lambda i, j: (0, 0, i))
    return pl.pallas_call(
        _peer_mix_kernel,
        grid=(t // tm, n_exp // EXPERT_BLOCK),
        in_specs=[pl.BlockSpec((tm, D_MODEL), lambda i, j: (i, 0)), pl.BlockSpec((tm, D_MODEL), lambda i, j: (i, 0)),
                  rowblk, rowblk, fullblk, fullblk,
                  pl.BlockSpec((EXPERT_BLOCK, D_MODEL), lambda i, j: (j, 0)),
                  pl.BlockSpec((D_MODEL, EXPERT_BLOCK), lambda i, j: (0, j))],
        out_specs=pl.BlockSpec((tm, D_MODEL), lambda i, j: (i, 0)),
        out_shape=jax.ShapeDtypeStruct((t, D_MODEL), F32),
        scratch_shapes=[pltpu.VMEM((EXPERT_BLOCK, tm), BF16), pltpu.VMEM((D_MODEL, tm), F32)],
        compiler_params=_cparams("parallel", "arbitrary"),
        name="peer_mix",
    )(h, xn, e1, mb, e2, bb, u, vt)


def _ple_kernel(h_ref, p_ref, nw_ref, wg_ref, wp_ref, fnw_ref, o_ref, *, final):
    h = h_ref[...]
    gate = 1.0 / (1.0 + jnp.exp(-_dot(_rms(h, nw_ref[...]).astype(BF16), wg_ref[...])))
    out = h + gate * _dot(p_ref[...].astype(BF16), wp_ref[...])
    if final:
        out = _rms(out, fnw_ref[...])
    o_ref[...] = out


def _ple(h, pemb, p, fnw, final, tm):
    t = h.shape[0]
    pd = pemb.shape[1]
    return pl.pallas_call(
        functools.partial(_ple_kernel, final=final),
        grid=(t // tm,),
        in_specs=[pl.BlockSpec((tm, D_MODEL), lambda i: (i, 0)), pl.BlockSpec((tm, pd), lambda i: (i, 0)),
                  _full((1, D_MODEL)), _full((D_MODEL, D_MODEL)), _full((pd, D_MODEL)), _full((1, D_MODEL))],
        out_specs=pl.BlockSpec((tm, D_MODEL), lambda i: (i, 0)),
        out_shape=jax.ShapeDtypeStruct((t, D_MODEL), F32),
        compiler_params=_cparams("parallel"),
        name="ple",
    )(h, pemb, p["ple_norm"], p["w_ple_gate"], p["w_ple_proj"], fnw)


def _row(v, width=None):
    v = v.reshape(1, -1).astype(F32)
    if width is not None and v.shape[1] < width:
        v = jnp.pad(v, ((0, 0), (0, width - v.shape[1])))
    return v


def _prep_even(w, e):
    win = w["w_in_e"][e]
    c0, c1, c2 = SSD_WIDTH, SSD_WIDTH + CONV_CH, SSD_WIDTH + CONV_CH + SSD_HEADS
    w_cat = jnp.concatenate([win[:, :c0], win[:, c0:c1], win[:, c2:], win[:, c1:c2],
                             jnp.zeros((D_MODEL, LANES - SSD_HEADS), F32)], axis=1).astype(BF16)
    expand = np.zeros((LANES, SSD_WIDTH), np.float32)
    for hh in range(SSD_HEADS):
        expand[hh, hh * SSD_HEAD_DIM:(hh + 1) * SSD_HEAD_DIM] = 1.0
    return dict(
        w_in=w_cat, w_dt_t=win[:, c1:c2].T.astype(BF16),
        conv_w=w["conv_w"][e], conv_b=_row(w["conv_b"][e]),
        dt_bias=_row(w["dt_bias"][e], LANES),
        dt_bias_t=jnp.broadcast_to(w["dt_bias"][e][:, None], (SSD_HEADS, SSD_BLOCK)),
        a_log=_row(w["a_log"][e], LANES),
        a_log_t=jnp.broadcast_to(w["a_log"][e][:, None], (SSD_HEADS, SSD_BLOCK)),
        d_skip=_row(jnp.repeat(w["d_skip"][e], SSD_HEAD_DIM)), ssd_norm_w=_row(w["ssd_norm_w"][e]),
        expand=jnp.asarray(expand, BF16),
        pool_w=w["pool_w"][e].astype(BF16), pool_scale=_row(w["pool_scale"][e]),
        w_out=w["w_out_e"][e].astype(BF16))


def _prep_odd(w, o):
    half = QK_ROPE // 2
    win = w["w_in_o"][o]
    w_kpe = win[:, Q_LORA + KV_LORA:]
    w_kpe_sw = jnp.concatenate([-w_kpe[:, half:], w_kpe[:, :half]], axis=1)
    zpad = jnp.zeros((D_MODEL, LANES - QK_ROPE), F32)
    w_cat = jnp.concatenate([win[:, :Q_LORA + KV_LORA], w_kpe, zpad, w_kpe_sw, zpad], axis=1).astype(BF16)
    wuq = w["w_uq"][o].reshape(Q_LORA, MLA_HEADS, QK_NOPE + QK_ROPE)
    nope, pe = wuq[..., :QK_NOPE], wuq[..., QK_NOPE:]
    pe_sw = jnp.concatenate([-pe[..., half:], pe[..., :half]], axis=-1)
    z32 = jnp.zeros((Q_LORA, MLA_HEADS, HEAD_SLOT - QK_NOPE - QK_ROPE), F32)
    w_qa = jnp.concatenate([nope, pe, z32], axis=-1).reshape(Q_LORA, QK_ROWS).astype(BF16)
    w_qb = jnp.concatenate([jnp.zeros_like(nope), pe_sw, z32], axis=-1).reshape(Q_LORA, QK_ROWS).astype(BF16)
    wukv = w["w_ukv"][o].reshape(KV_LORA, MLA_HEADS, QK_NOPE + V_DIM)
    w_k = jnp.concatenate([wukv[..., :QK_NOPE], jnp.zeros((KV_LORA, MLA_HEADS, HEAD_SLOT - QK_NOPE), F32)],
                          axis=-1).reshape(KV_LORA, QK_ROWS).astype(BF16)
    w_vt = wukv[..., QK_NOPE:].reshape(KV_LORA, MLA_HEADS * V_DIM).T.astype(BF16)
    r128 = np.zeros((half, LANES), np.float32)
    rq = np.zeros((half, QK_ROWS), np.float32)
    rq_mask = np.zeros((1, QK_ROWS), np.float32)
    r_k = np.zeros((QK_ROPE, QK_ROWS), np.float32)
    for i in range(half):
        r128[i, i] = r128[i, half + i] = 1.0
    for hh in range(MLA_HEADS):
        base = hh * HEAD_SLOT
        rq_mask[0, base:base + QK_NOPE] = 1.0
        for i in range(half):
            rq[i, base + QK_NOPE + i] = rq[i, base + QK_NOPE + half + i] = 1.0
        for i in range(QK_ROPE):
            r_k[i, base + QK_NOPE + i] = 1.0
    return dict(
        w_in=w_cat, q_norm=_row(w["q_norm"][o]), kv_norm=_row(w["kv_norm"][o]), w_qa=w_qa, w_qb=w_qb,
        r128=jnp.asarray(r128, BF16), rq=jnp.asarray(rq, BF16), rq_mask=jnp.asarray(rq_mask),
        w_k=w_k, r_k=jnp.asarray(r_k, BF16), w_vt=w_vt, w_out=w["w_out_o"][o].astype(BF16))


def _hi_lo(x):
    hi = x.astype(BF16)
    return hi, (x - hi.astype(F32)).astype(BF16)


def _prep_common(w, i):
    wq_hi, wq_lo = _hi_lo(w["peer_wq"][i].T)
    keys_hi, keys_lo = _hi_lo(w["peer_keys"][i].reshape(2 * PEER_HEADS, PEER_NKEYS, PEER_HALF))
    sel_a = np.zeros((STAIR_ROWS, PEER_TOPK), np.float32)
    sel_b = np.zeros((STAIR_ROWS, PEER_TOPK), np.float32)
    for row_i, (a, b) in enumerate(_STAIR):
        sel_a[row_i, a] = 1.0
        sel_b[row_i, b] = 1.0
    return dict(
        norm_mix=_row(w["norm_mix"][i]), norm_ffn=_row(w["norm_ffn"][i]), ple_norm=_row(w["ple_norm"][i]),
        wq_hi=wq_hi, wq_lo=wq_lo, keys_hi=keys_hi, keys_lo=keys_lo,
        sel_a=jnp.asarray(sel_a, BF16), sel_b=jnp.asarray(sel_b, BF16),
        peer_u=w["peer_u"][i].astype(BF16), peer_vt=w["peer_v"][i].T.astype(BF16),
        w_ple_gate=w["w_ple_gate"][i].astype(BF16), w_ple_proj=w["w_ple_proj"][i].astype(BF16))


def _pad_axis(x, axis, size):
    if x.shape[axis] == size:
        return x
    pads = [(0, 0)] * x.ndim
    pads[axis] = (0, size - x.shape[axis])
    return jnp.pad(x, pads)


def _round_up(n, m):
    return -(-n // m) * m


def _even_layer(h, conv_st, ssm_st, pool_st, pos0, p, cfg):
    b, l, _ = h.shape
    z, xbc, u, dt, dtt = _even_in(h.reshape(b * l, D_MODEL), p["norm_mix"], p["w_in"], p["w_dt_t"], cfg["tm"])
    z, xbc, u, dt = (a.reshape(b, l, -1) for a in (z, xbc, u, dt))
    dtt = dtt.reshape(SSD_HEADS, b, l).transpose(1, 0, 2)
    lp = _round_up(l, SSD_BLOCK)
    r = SSD_HEADS // SSD_GROUPS
    s0 = ssm_st.astype(F32).reshape(b, SSD_GROUPS, r, SSD_HEAD_DIM, SSD_STATE).transpose(0, 1, 4, 2, 3)
    s0 = s0.reshape(b, SSD_GROUPS, SSD_STATE, r * SSD_HEAD_DIM)
    hist = jnp.pad(conv_st.astype(F32), ((0, 0), (8 - (CONV_WIDTH - 1), 0), (0, 0)))
    y, s_fin = _ssd(_pad_axis(xbc, 1, lp), _pad_axis(z, 1, lp), _pad_axis(dt, 1, lp), _pad_axis(dtt, 2, lp),
                    hist, s0, p, l)
    new_ssm = s_fin.reshape(b, SSD_GROUPS, SSD_STATE, r, SSD_HEAD_DIM).transpose(0, 1, 3, 4, 2)
    new_ssm = new_ssm.reshape(b, SSD_HEADS, SSD_HEAD_DIM, SSD_STATE)
    new_conv = jnp.concatenate([conv_st.astype(F32), xbc], axis=1)[:, -(CONV_WIDTH - 1):]
    new_pool = jnp.concatenate([pool_st.astype(F32), u], axis=1)[:, -POOL_HIST:]
    phist = jnp.pad(pool_st.astype(F32), ((0, 0), (16 - POOL_HIST, 0), (0, 0)))
    h_new = _pool_out(u, phist, y[:, :l], h, p, pos0, cfg["tm_seq"])
    return h_new, new_conv, new_ssm, new_pool


def _odd_layer(h, ckv_hist, kpe_hist, pos0, p, cfg):
    b, l, _ = h.shape
    half = QK_ROPE // 2
    pos = (pos0 + jnp.arange(l)).astype(F32)
    inv = ROPE_THETA ** (-jnp.arange(half, dtype=F32) / half)
    ang = pos[:, None] * inv[None, :]
    cos = jnp.tile(jnp.cos(ang), (b, 1))
    sin = jnp.tile(jnp.sin(ang), (b, 1))
    hf = h.reshape(b * l, D_MODEL)
    ckv, kpe, qt = _odd_in(hf, cos, sin, p, cfg["tm"])
    ckv = ckv.reshape(b, l, KV_LORA)
    kpe = kpe.reshape(b, l, QK_ROPE)
    ckv_all = jnp.concatenate([ckv_hist.astype(F32), ckv], axis=1)
    kpe_all = jnp.concatenate([kpe_hist.astype(F32), kpe], axis=1)
    n_keys = ckv_all.shape[1]
    causal = ckv_hist.shape[1] == 0
    if causal:
        tq, tk, nk_pad, lq = cfg["tq"], cfg["tk"], n_keys, l
    else:
        nk_pad = _round_up(n_keys, LANES)
        tq, tk, lq = LANES, nk_pad, _round_up(l, LANES)
    k, vt = _kv_expand(_pad_axis(ckv_all, 1, nk_pad), _pad_axis(kpe_all, 1, nk_pad), p, cfg["tm_kv"])
    qt = _pad_axis(qt.reshape(QK_ROWS, b, l).transpose(1, 0, 2), 2, lq)
    ot = _attention(qt, k, vt, tq, tk, causal, None if causal else n_keys)
    ot = ot[:, :, :l].transpose(1, 0, 2).reshape(MLA_HEADS * V_DIM, b * l)
    h_new = _attn_out(ot, hf, p["w_out"], cfg["tm"]).reshape(b, l, D_MODEL)
    return h_new, ckv, kpe


def _prep_layers(w):
    layers = []
    for i in range(DEPTH):
        pc = _prep_common(w, i)
        layers.append(dict(pc, **(_prep_even(w, i // 2) if i % 2 == 0 else _prep_odd(w, i // 2))))
    return layers


def _trunk(x, pemb, conv_st, ssm_st, pool_st, ckv_h, kpe_h, pos0, layers, fnw, cfg):
    b, l, _ = x.shape
    h = x.astype(F32)
    convs, ssms, pools, ckvs, kpes = [], [], [], [], []
    for i in range(DEPTH):
        pc = layers[i]
        if i % 2 == 0:
            e = i // 2
            h, c_new, s_new, p_new = _even_layer(h, conv_st[e], ssm_st[e], pool_st[e], pos0, pc, cfg)
            convs.append(c_new)
            ssms.append(s_new)
            pools.append(p_new)
        else:
            o = i // 2
            h, ckv_new, kpe_new = _odd_layer(h, ckv_h[o], kpe_h[o], pos0, pc, cfg)
            ckvs.append(ckv_new)
            kpes.append(kpe_new)
        hf = h.reshape(b * l, D_MODEL)
        xn, e1, mb, e2, bb = _peer_route(hf, pc, cfg["tm_route"])
        hf = _peer_mix(hf, xn, e1, mb, e2, bb, pc["peer_u"], pc["peer_vt"], cfg["tm_mix"])
        hf = _ple(hf, pemb[i].reshape(b * l, -1), pc, fnw, i == DEPTH - 1, cfg["tm"])
        h = hf.reshape(b, l, D_MODEL)
    return h, jnp.stack(convs), jnp.stack(ssms), jnp.stack(pools), jnp.stack(ckvs), jnp.stack(kpes)


def _tile(n, pref):
    t = min(pref, n)
    while n % t:
        t -= 8
    return t


def _config(b, l):
    t = b * l
    return dict(tm=_tile(t, 512), tm_seq=_tile(l, 512), tm_route=_tile(t, 256), tm_mix=_tile(t, 512),
                tq=_tile(l, 512), tk=_tile(l, 512), tm_kv=LANES)


def kernel(x_prompt, x_sample, state_conv, state_ssm, state_pool, cache_ckv, cache_kpe, p_prompt, p_sample,
           norm_mix, norm_ffn, ple_norm, final_norm, w_in_e, conv_w, conv_b, dt_bias, a_log, d_skip, ssd_norm_w,
           pool_w, pool_scale, w_out_e, w_in_o, q_norm, kv_norm, w_uq, w_ukv, w_out_o, peer_wq, peer_keys, peer_u,
           peer_v, w_ple_proj, w_ple_gate):
    w = dict(norm_mix=norm_mix, norm_ffn=norm_ffn, ple_norm=ple_norm, final_norm=final_norm, w_in_e=w_in_e,
             conv_w=conv_w, conv_b=conv_b, dt_bias=dt_bias, a_log=a_log, d_skip=d_skip, ssd_norm_w=ssd_norm_w,
             pool_w=pool_w, pool_scale=pool_scale, w_out_e=w_out_e, w_in_o=w_in_o, q_norm=q_norm, kv_norm=kv_norm,
             w_uq=w_uq, w_ukv=w_ukv, w_out_o=w_out_o, peer_wq=peer_wq, peer_keys=peer_keys, peer_u=peer_u,
             peer_v=peer_v, w_ple_proj=w_ple_proj, w_ple_gate=w_ple_gate)
    n_even, n_odd = (DEPTH + 1) // 2, DEPTH // 2
    b0, l0, _ = x_prompt.shape
    conv0 = jnp.zeros((n_even, b0, CONV_WIDTH - 1, CONV_CH), F32)
    ssm0 = jnp.zeros((n_even, b0, SSD_HEADS, SSD_HEAD_DIM, SSD_STATE), F32)
    pool0 = jnp.zeros((n_even, b0, POOL_HIST, SSD_WIDTH), F32)
    ckv0 = jnp.zeros((n_odd, b0, 0, KV_LORA), F32)
    kpe0 = jnp.zeros((n_odd, b0, 0, QK_ROPE), F32)
    layers = _prep_layers(w)
    fnw = _row(final_norm)
    outs_p = _trunk(x_prompt, p_prompt, conv0, ssm0, pool0, ckv0, kpe0, 0, layers, fnw, _config(b0, l0))
    b1, l1, _ = x_sample.shape
    pos0 = cache_ckv.shape[2]
    outs_s = _trunk(x_sample, p_sample, state_conv, state_ssm, state_pool, cache_ckv, cache_kpe, pos0, layers, fnw,
                    _config(b1, l1))
    return (outs_p[0], outs_s[0]) + outs_p[1:] + outs_s[1:]
```

```python
import functools

import numpy as np
import jax
import jax.numpy as jnp
from jax import lax
from jax.experimental import pallas as pl
from jax.experimental.pallas import tpu as pltpu

F32 = jnp.float32
BF16 = jnp.bfloat16
I32 = jnp.int32

D_MODEL = 1024
DEPTH = 4
CHUNK = 64
EPS = 1e-6
SSD_HEAD_DIM = 64
SSD_HEADS = 16
SSD_WIDTH = 1024
SSD_GROUPS = 2
SSD_STATE = 128
CONV_WIDTH = 4
CONV_CH = 1536
POOL_WINDOWS = (2, 4, 8, 16)
POOL_GROUP_DIM = 256
POOL_HIST = 15
QK_NOPE = 64
QK_ROPE = 32
V_DIM = 64
MLA_HEADS = 16
Q_LORA = 256
KV_LORA = 256
ROPE_THETA = 10000.0
PEER_HEADS = 8
PEER_NKEYS = 128
PEER_HALF = 128
PEER_TOPK = 16

LANES = 128
VMEM_LIMIT_BYTES = 56 * 1024 * 1024
SSD_BLOCK = 128
HEAD_SLOT = 128

NEG_INF = float("-inf")


def _cparams(*sem):
    return pltpu.CompilerParams(dimension_semantics=sem, vmem_limit_bytes=VMEM_LIMIT_BYTES)


def _dot(a, b):
    return jnp.dot(a, b, preferred_element_type=F32)


def _dot_nt(a, b):
    return lax.dot_general(a, b, (((1,), (1,)), ((), ())), preferred_element_type=F32)


def _split3(x):
    hi = x.astype(BF16)
    r = x - hi.astype(F32)
    mid = r.astype(BF16)
    lo = (r - mid.astype(F32)).astype(BF16)
    return hi, mid, lo


def _sel_rhs(x, sel):
    hi, mid, lo = _split3(x)
    return _dot(hi, sel) + _dot(mid, sel) + _dot(lo, sel)


def _sel_lhs(sel, x):
    hi, mid, lo = _split3(x)
    return _dot(sel, hi) + _dot(sel, mid) + _dot(sel, lo)


def _rms(x, w):
    return x * lax.rsqrt(jnp.mean(x * x, axis=-1, keepdims=True) + EPS) * w


def _silu(x):
    return x * (1.0 / (1.0 + jnp.exp(-x)))


def _softplus(x):
    return jnp.maximum(x, 0.0) + jnp.log(1.0 + jnp.exp(-jnp.abs(x)))


def _full(shape):
    nd = len(shape)
    return pl.BlockSpec(shape, lambda *_: (0,) * nd)


EVEN_COLS = SSD_WIDTH + CONV_CH + SSD_WIDTH + LANES


def _even_in_kernel(h_ref, nw_ref, w_ref, wdt_ref, z_ref, xbc_ref, u_ref, dt_ref, dtt_ref):
    xn = _rms(h_ref[...], nw_ref[...]).astype(BF16)
    proj = _dot(xn, w_ref[...])
    z_ref[...] = proj[:, :SSD_WIDTH]
    xbc_ref[...] = proj[:, SSD_WIDTH:SSD_WIDTH + CONV_CH]
    u_ref[...] = proj[:, SSD_WIDTH + CONV_CH:2 * SSD_WIDTH + CONV_CH]
    dt_ref[...] = proj[:, 2 * SSD_WIDTH + CONV_CH:]
    dtt_ref[...] = _dot_nt(wdt_ref[...], xn)


def _even_in(h, nw, w, wdt, tm):
    t = h.shape[0]
    row = lambda n: pl.BlockSpec((tm, n), lambda i: (i, 0))
    return pl.pallas_call(
        _even_in_kernel,
        grid=(t // tm,),
        in_specs=[row(D_MODEL), _full((1, D_MODEL)), _full((D_MODEL, EVEN_COLS)), _full((SSD_HEADS, D_MODEL))],
        out_specs=[row(SSD_WIDTH), row(CONV_CH), row(SSD_WIDTH), row(LANES),
                   pl.BlockSpec((SSD_HEADS, tm), lambda i: (0, i))],
        out_shape=[jax.ShapeDtypeStruct((t, SSD_WIDTH), F32), jax.ShapeDtypeStruct((t, CONV_CH), F32),
                   jax.ShapeDtypeStruct((t, SSD_WIDTH), F32), jax.ShapeDtypeStruct((t, LANES), F32),
                   jax.ShapeDtypeStruct((SSD_HEADS, t), F32)],
        compiler_params=_cparams("parallel"),
        name="even_in",
    )(h, nw, w, wdt)


def _ssd_kernel(xbc_ref, z_ref, dt_ref, dtt_ref, hist_ref, s0_ref, cw_ref, cb_ref, bias_ref, biast_ref,
                a_ref, at_ref, dskip_ref, nw_ref, expand_ref, y_ref, sfin_ref, ext_ref, st_ref, *, valid_len):
    q = SSD_BLOCK
    c = pl.program_id(1)

    @pl.when(c == 0)
    def _():
        ext_ref[0:8, :] = hist_ref[0]
        st_ref[...] = s0_ref[0]

    ext_ref[8:8 + q, :] = xbc_ref[0]
    conv = cb_ref[...] + cw_ref[0:1, :] * ext_ref[5:5 + q, :]
    for j in range(1, CONV_WIDTH):
        conv = conv + cw_ref[j:j + 1, :] * ext_ref[5 + j:5 + j + q, :]
    ext_ref[0:8, :] = ext_ref[q:q + 8, :]
    xbc = _silu(conv)
    xs = xbc[:, :SSD_WIDTH]
    xs_b = xs.astype(BF16)

    row_i = lax.broadcasted_iota(I32, (q, q), 0)
    col_i = lax.broadcasted_iota(I32, (q, q), 1)
    tril = (row_i >= col_i).astype(BF16)
    triu = (row_i <= col_i).astype(BF16)
    ones = jnp.ones((q, q), BF16)
    tok_ok = (c * q + lax.broadcasted_iota(I32, (q, LANES), 0)) < valid_len
    dt_tok = jnp.where(tok_ok, _softplus(dt_ref[0] + bias_ref[...]), 0.0)
    cs_tok = _sel_lhs(tril, dt_tok * -jnp.exp(a_ref[...]))
    lane_ok = (c * q + lax.broadcasted_iota(I32, (SSD_HEADS, q), 1)) < valid_len
    dt_t = jnp.where(lane_ok, _softplus(dtt_ref[0] + biast_ref[...]), 0.0)
    da_t = dt_t * -jnp.exp(at_ref[...])
    cs_t = _sel_rhs(da_t, triu)
    tot_t = _sel_rhs(da_t, ones)
    w_t = dt_t * jnp.exp(tot_t - cs_t)
    cs_exp = _sel_rhs(cs_tok, expand_ref[...])
    ecs = jnp.exp(cs_exp)
    dchunk = ecs[q - 1:q, :]

    lane_lo = lax.broadcasted_iota(I32, (q, LANES), 1) < SSD_HEAD_DIM
    causal = row_i >= col_i
    r = SSD_HEADS // SSD_GROUPS
    gw = r * SSD_HEAD_DIM
    y_parts = []
    for g in range(SSD_GROUPS):
        bm = xbc[:, SSD_WIDTH + g * SSD_STATE:SSD_WIDTH + (g + 1) * SSD_STATE]
        cm = xbc[:, SSD_WIDTH + (SSD_GROUPS + g) * SSD_STATE:SSD_WIDTH + (SSD_GROUPS + g + 1) * SSD_STATE]
        cm_b = cm.astype(BF16)
        cb = _dot_nt(cm_b, bm.astype(BF16))
        bm_t = bm.T
        st_g = st_ref[g]
        y_off = _dot(cm_b, st_g.astype(BF16)) * ecs[:, g * gw:(g + 1) * gw]
        yd, up = [], []
        for pr in range(r // 2):
            res_y, res_u = [], []
            xs_pair = xs_b[:, g * gw + pr * LANES:g * gw + (pr + 1) * LANES]
            for k in range(2):
                hh = g * r + 2 * pr + k
                seg = cs_tok[:, hh:hh + 1] - cs_t[hh:hh + 1, :]
                lmat = jnp.where(causal, jnp.exp(seg), 0.0)
                m = (cb * lmat * dt_t[hh:hh + 1, :]).astype(BF16)
                res_y.append(_dot(m, xs_pair))
                res_u.append(_dot((bm_t * w_t[hh:hh + 1, :]).astype(BF16), xs_pair))
            yd.append(jnp.where(lane_lo, res_y[0], res_y[1]))
            up.append(jnp.where(lane_lo, res_u[0], res_u[1]))
        y_parts.append(jnp.concatenate(yd, axis=1) + y_off)
        st_ref[g] = dchunk[:, g * gw:(g + 1) * gw] * st_g + jnp.concatenate(up, axis=1)

    y = jnp.concatenate(y_parts, axis=1) + dskip_ref[...] * xs
    y = y * _silu(z_ref[0])
    outs = []
    for g in range(SSD_GROUPS):
        yg = y[:, g * gw:(g + 1) * gw]
        outs.append(yg * lax.rsqrt(jnp.mean(yg * yg, axis=-1, keepdims=True) + EPS))
    y_ref[0] = (jnp.concatenate(outs, axis=1) * nw_ref[...]).astype(y_ref.dtype)

    @pl.when(c == pl.num_programs(1) - 1)
    def _():
        sfin_ref[0] = st_ref[...]


def _ssd(xbc, z, dt, dtt, hist, s0, p, valid_len):
    b, l, _ = xbc.shape
    q = SSD_BLOCK
    n_state = SSD_GROUPS * SSD_STATE
    gw = SSD_WIDTH // SSD_GROUPS
    blk = lambda n: pl.BlockSpec((1, q, n), lambda i, c: (i, c, 0))
    per_b = lambda s: pl.BlockSpec((1,) + s, lambda i, c: (i,) + (0,) * len(s))
    return pl.pallas_call(
        functools.partial(_ssd_kernel, valid_len=valid_len),
        grid=(b, l // q),
        in_specs=[blk(CONV_CH), blk(SSD_WIDTH), blk(LANES),
                  pl.BlockSpec((1, SSD_HEADS, q), lambda i, c: (i, 0, c)),
                  per_b((8, CONV_CH)), per_b((SSD_GROUPS, SSD_STATE, gw)),
                  _full((CONV_WIDTH, CONV_CH)), _full((1, CONV_CH)), _full((1, LANES)), _full((SSD_HEADS, q)),
                  _full((1, LANES)), _full((SSD_HEADS, q)), _full((1, SSD_WIDTH)), _full((1, SSD_WIDTH)),
                  _full((LANES, SSD_WIDTH))],
        out_specs=[blk(SSD_WIDTH), per_b((SSD_GROUPS, SSD_STATE, gw))],
        out_shape=[jax.ShapeDtypeStruct((b, l, SSD_WIDTH), BF16),
                   jax.ShapeDtypeStruct((b, SSD_GROUPS, SSD_STATE, gw), F32)],
        scratch_shapes=[pltpu.VMEM((q + 8, CONV_CH), F32), pltpu.VMEM((SSD_GROUPS, SSD_STATE, gw), F32)],
        compiler_params=_cparams("parallel", "arbitrary"),
        name="ssd_scan",
    )(xbc, z, dt, dtt, hist, s0, p["conv_w"], p["conv_b"], p["dt_bias"], p["dt_bias_t"], p["a_log"], p["a_log_t"],
      p["d_skip"], p["ssd_norm_w"], p["expand"])


def _pool_out_kernel(u_ref, hist_ref, y_ref, h_ref, pw_ref, ps_ref, wo_ref, o_ref, ext_ref, *, pos0, tm):
    i = pl.program_id(1)

    @pl.when(i == 0)
    def _():
        ext_ref[0:16, :] = hist_ref[0]

    ext_ref[16:16 + tm, :] = u_ref[0]
    pos = pos0 + i * tm + lax.broadcasted_iota(I32, (tm, POOL_GROUP_DIM), 0)
    acc = h_ref[0] + _dot(y_ref[0], wo_ref[0:SSD_WIDTH, :])
    yps = []
    for gi, wsz in enumerate(POOL_WINDOWS):
        c0 = gi * POOL_GROUP_DIM
        cur = ext_ref[16:16 + tm, c0:c0 + POOL_GROUP_DIM]
        tot = cur
        for j in range(1, wsz):
            tot = tot + ext_ref[16 - j:16 - j + tm, c0:c0 + POOL_GROUP_DIM]
        cnt = jnp.minimum(pos + 1, wsz).astype(F32)
        pooled = tot / cnt - cur
        yps.append(_dot(pooled.astype(BF16), pw_ref[gi]))
    yp = (jnp.concatenate(yps, axis=1) * ps_ref[...]).astype(BF16)
    o_ref[0] = acc + _dot(yp, wo_ref[SSD_WIDTH:, :])
    ext_ref[0:16, :] = ext_ref[tm:tm + 16, :]


def _pool_out(u, hist, y, h, p, pos0, tm):
    b, l, _ = u.shape
    blk = lambda n: pl.BlockSpec((1, tm, n), lambda bi, i: (bi, i, 0))
    return pl.pallas_call(
        functools.partial(_pool_out_kernel, pos0=pos0, tm=tm),
        grid=(b, l // tm),
        in_specs=[blk(SSD_WIDTH), pl.BlockSpec((1, 16, SSD_WIDTH), lambda bi, i: (bi, 0, 0)), blk(SSD_WIDTH),
                  blk(D_MODEL), _full((len(POOL_WINDOWS), POOL_GROUP_DIM, POOL_GROUP_DIM)), _full((1, SSD_WIDTH)),
                  _full((2 * SSD_WIDTH, D_MODEL))],
        out_specs=blk(D_MODEL),
        out_shape=jax.ShapeDtypeStruct((b, l, D_MODEL), F32),
        scratch_shapes=[pltpu.VMEM((tm + 16, SSD_WIDTH), F32)],
        compiler_params=_cparams("parallel", "arbitrary"),
        name="pool_out",
    )(u, hist, y, h, p["pool_w"], p["pool_scale"], p["w_out"])


ODD_COLS = Q_LORA + KV_LORA + 2 * LANES
QK_ROWS = MLA_HEADS * HEAD_SLOT


def _odd_in_kernel(h_ref, nw_ref, w_ref, qn_ref, kvn_ref, wa_ref, wb_ref, cos_ref, sin_ref, r128_ref, rq_ref,
                   rqm_ref, ckv_ref, kpe_ref, qt_ref, *, scale):
    xn = _rms(h_ref[...], nw_ref[...]).astype(BF16)
    proj = _dot(xn, w_ref[...])
    cq = _rms(proj[:, :Q_LORA], qn_ref[...]).astype(BF16)
    ckv_ref[...] = _rms(proj[:, Q_LORA:Q_LORA + KV_LORA], kvn_ref[...])
    cos = cos_ref[...]
    sin = sin_ref[...]
    kpe = proj[:, Q_LORA + KV_LORA:Q_LORA + KV_LORA + LANES]
    kpe_sw = proj[:, Q_LORA + KV_LORA + LANES:]
    kpe_rot = kpe * _sel_rhs(cos, r128_ref[...]) + kpe_sw * _sel_rhs(sin, r128_ref[...])
    kpe_ref[...] = kpe_rot[:, :QK_ROPE]
    cos_q = (_sel_rhs(cos, rq_ref[...]) + rqm_ref[...]) * scale
    sin_q = _sel_rhs(sin, rq_ref[...]) * scale
    q = _dot(cq, wa_ref[...]) * cos_q + _dot(cq, wb_ref[...]) * sin_q
    qt_ref[...] = q.T.astype(qt_ref.dtype)


def _odd_in(h, cos, sin, p, tm):
    t = h.shape[0]
    row = lambda n: pl.BlockSpec((tm, n), lambda i: (i, 0))
    half = QK_ROPE // 2
    scale = float((QK_NOPE + QK_ROPE) ** -0.5 * np.log2(np.e))
    return pl.pallas_call(
        functools.partial(_odd_in_kernel, scale=scale),
        grid=(t // tm,),
        in_specs=[row(D_MODEL), _full((1, D_MODEL)), _full((D_MODEL, ODD_COLS)), _full((1, Q_LORA)),
                  _full((1, KV_LORA)), _full((Q_LORA, QK_ROWS)), _full((Q_LORA, QK_ROWS)), row(half), row(half),
                  _full((half, LANES)), _full((half, QK_ROWS)), _full((1, QK_ROWS))],
        out_specs=[row(KV_LORA), row(QK_ROPE), pl.BlockSpec((QK_ROWS, tm), lambda i: (0, i))],
        out_shape=[jax.ShapeDtypeStruct((t, KV_LORA), F32), jax.ShapeDtypeStruct((t, QK_ROPE), F32),
                   jax.ShapeDtypeStruct((QK_ROWS, t), BF16)],
        compiler_params=_cparams("parallel"),
        name="odd_in",
    )(h, p["norm_mix"], p["w_in"], p["q_norm"], p["kv_norm"], p["w_qa"], p["w_qb"], cos, sin, p["r128"], p["rq"],
      p["rq_mask"])


def _kv_expand_kernel(ckv_ref, kpe_ref, wk_ref, rk_ref, wvt_ref, k_ref, vt_ref):
    ckv = ckv_ref[0].astype(BF16)
    k_ref[0] = (_dot(ckv, wk_ref[...]) + _dot(kpe_ref[0].astype(BF16), rk_ref[...])).astype(k_ref.dtype)
    vt_ref[0] = _dot_nt(wvt_ref[...], ckv).astype(vt_ref.dtype)


def _kv_expand(ckv, kpe, p, tm):
    b, t, _ = ckv.shape
    vrows = MLA_HEADS * V_DIM
    return pl.pallas_call(
        _kv_expand_kernel,
        grid=(b, t // tm),
        in_specs=[pl.BlockSpec((1, tm, KV_LORA), lambda bi, i: (bi, i, 0)),
                  pl.BlockSpec((1, tm, QK_ROPE), lambda bi, i: (bi, i, 0)),
                  _full((KV_LORA, QK_ROWS)), _full((QK_ROPE, QK_ROWS)), _full((vrows, KV_LORA))],
        out_specs=[pl.BlockSpec((1, tm, QK_ROWS), lambda bi, i: (bi, i, 0)),
                   pl.BlockSpec((1, vrows, tm), lambda bi, i: (bi, 0, i))],
        out_shape=[jax.ShapeDtypeStruct((b, t, QK_ROWS), BF16), jax.ShapeDtypeStruct((b, vrows, t), BF16)],
        compiler_params=_cparams("parallel", "parallel"),
        name="kv_expand",
    )(ckv, kpe, p["w_k"], p["r_k"], p["w_vt"])


ATTN_HEADS_PER_STEP = 8
ATTN_SUB = 256
ATTN_LOOKAHEAD = 5


def _attn_kernel(qi_ref, ki_ref, qt_ref, k_ref, vt_ref, o_ref, m_ref, l_ref, acc_ref, *, tq, tk, causal,
                 kv_valid):
    s_idx = pl.program_id(2)
    qi = qi_ref[s_idx]
    ki = ki_ref[s_idx]
    nh = ATTN_HEADS_PER_STEP

    @pl.when(ki == 0)
    def _():
        m_ref[...] = jnp.full(m_ref.shape, NEG_INF, F32)
        l_ref[...] = jnp.zeros(l_ref.shape, F32)
        acc_ref[...] = jnp.zeros(acc_ref.shape, F32)

    sk, sq = min(ATTN_SUB, tk), min(ATTN_SUB, tq)

    def step(diagonal):
        items = []
        for ks in range(tk // sk):
            for hh in range(nh):
                for qs in range(tq // sq):
                    if causal and diagonal and ks * sk >= (qs + 1) * sq:
                        continue
                    if not causal and kv_valid is not None and ks * sk >= kv_valid:
                        continue
                    items.append((ks, hh, qs))

        def needs_mask(ks, qs):
            if causal:
                return diagonal and (ks + 1) * sk > qs * sq
            return kv_valid is not None and (ks + 1) * sk > kv_valid

        def scores(ks, hh, qs):
            return _dot(k_ref[0, ks * sk:(ks + 1) * sk, hh * HEAD_SLOT:(hh + 1) * HEAD_SLOT],
                        qt_ref[0, hh * HEAD_SLOT:(hh + 1) * HEAD_SLOT, qs * sq:(qs + 1) * sq])

        state = {}
        for hh in range(nh):
            for qs in range(tq // sq):
                cols = slice(qs * sq, (qs + 1) * sq)
                state[hh, qs] = (m_ref[hh:hh + 1, cols], l_ref[hh:hh + 1, cols],
                                 acc_ref[hh * V_DIM:(hh + 1) * V_DIM, cols])
        pending = [scores(*it) for it in items[:ATTN_LOOKAHEAD]]
        for idx, (ks, hh, qs) in enumerate(items):
            st = pending.pop(0)
            if idx + ATTN_LOOKAHEAD < len(items):
                pending.append(scores(*items[idx + ATTN_LOOKAHEAD]))
            if needs_mask(ks, qs):
                kpos = ks * sk + lax.broadcasted_iota(I32, (sk, sq), 0)
                if causal:
                    qpos = qs * sq + lax.broadcasted_iota(I32, (sk, sq), 1)
                    ok = (kpos // CHUNK) <= (qpos // CHUNK)
                else:
                    ok = kpos < kv_valid
                st = jnp.where(ok, st, NEG_INF)
            m_prev, l_prev, acc_prev = state[hh, qs]
            m_new = jnp.maximum(m_prev, jnp.max(st, axis=0, keepdims=True))
            alpha = jnp.exp2(m_prev - m_new)
            pt = jnp.exp2(st - m_new)
            l_new = alpha * l_prev + jnp.sum(pt, axis=0, keepdims=True)
            pv = _dot(vt_ref[0, hh * V_DIM:(hh + 1) * V_DIM, ks * sk:(ks + 1) * sk], pt.astype(BF16))
            state[hh, qs] = (m_new, l_new, alpha * acc_prev + pv)
        for (hh, qs), (m_new, l_new, acc_new) in state.items():
            cols = slice(qs * sq, (qs + 1) * sq)
            m_ref[hh:hh + 1, cols] = m_new
            l_ref[hh:hh + 1, cols] = l_new
            acc_ref[hh * V_DIM:(hh + 1) * V_DIM, cols] = acc_new

    def finish():
        for hh in range(nh):
            vrows = slice(hh * V_DIM, (hh + 1) * V_DIM)
            o_ref[0, vrows, :] = (acc_ref[vrows, :] / l_ref[hh:hh + 1, :]).astype(o_ref.dtype)

    if causal:
        last = ((qi + 1) * tq - 1) // tk

        @pl.when(ki < last)
        def _():
            step(False)

        @pl.when(ki == last)
        def _():
            step(True)
            finish()
    else:
        step(False)
        finish()


def _attention(qt, k, vt, tq, tk, causal, kv_valid):
    b, _, t_q = qt.shape
    t_k = k.shape[1]
    nq, nk = t_q // tq, t_k // tk
    nh = ATTN_HEADS_PER_STEP
    if causal:
        assert tk == tq
        pairs = [(a, c) for a in range(nq) for c in range(((a + 1) * tq - 1) // tk + 1)]
    else:
        assert nk == 1
        pairs = [(a, 0) for a in range(nq)]
    qi = jnp.asarray(np.array([a for a, _ in pairs], np.int32))
    ki = jnp.asarray(np.array([c for _, c in pairs], np.int32))
    grid_spec = pltpu.PrefetchScalarGridSpec(
        num_scalar_prefetch=2,
        grid=(b, MLA_HEADS // nh, len(pairs)),
        in_specs=[pl.BlockSpec((1, nh * HEAD_SLOT, tq), lambda bi, hi, s, qi_r, ki_r: (bi, hi, qi_r[s])),
                  pl.BlockSpec((1, tk, nh * HEAD_SLOT), lambda bi, hi, s, qi_r, ki_r: (bi, ki_r[s], hi)),
                  pl.BlockSpec((1, nh * V_DIM, tk), lambda bi, hi, s, qi_r, ki_r: (bi, hi, ki_r[s]))],
        out_specs=pl.BlockSpec((1, nh * V_DIM, tq), lambda bi, hi, s, qi_r, ki_r: (bi, hi, qi_r[s])),
        scratch_shapes=[pltpu.VMEM((nh, tq), F32), pltpu.VMEM((nh, tq), F32), pltpu.VMEM((nh * V_DIM, tq), F32)],
    )
    return pl.pallas_call(
        functools.partial(_attn_kernel, tq=tq, tk=tk, causal=causal, kv_valid=kv_valid),
        grid_spec=grid_spec,
        out_shape=jax.ShapeDtypeStruct((b, MLA_HEADS * V_DIM, t_q), F32),
        compiler_params=_cparams("parallel", "parallel", "arbitrary"),
        name="attention",
    )(qi, ki, qt, k, vt)


def _attn_out_kernel(ot_ref, h_ref, wo_ref, o_ref):
    o_ref[...] = h_ref[...] + _dot(ot_ref[...].T.astype(BF16), wo_ref[...])


def _attn_out(ot, h, wo, tm):
    t = h.shape[0]
    return pl.pallas_call(
        _attn_out_kernel,
        grid=(t // tm,),
        in_specs=[pl.BlockSpec((MLA_HEADS * V_DIM, tm), lambda i: (0, i)),
                  pl.BlockSpec((tm, D_MODEL), lambda i: (i, 0)), _full((MLA_HEADS * V_DIM, D_MODEL))],
        out_specs=pl.BlockSpec((tm, D_MODEL), lambda i: (i, 0)),
        out_shape=jax.ShapeDtypeStruct((t, D_MODEL), F32),
        compiler_params=_cparams("parallel"),
        name="attn_out",
    )(ot, h, wo)


_STAIR = [(a, b) for a in range(PEER_TOPK) for b in range(PEER_TOPK) if (a + 1) * (b + 1) <= PEER_TOPK]
STAIR_ROWS = 56
NOT_RANKED = PEER_TOPK
PAD_FLAT_INDEX = PEER_TOPK * PEER_TOPK


def _top_values(s):
    vals = []
    x = s
    for _ in range(PEER_TOPK):
        m = jnp.max(x, axis=0, keepdims=True)
        x = jnp.where(x == m, NEG_INF, x)
        vals.append(m)
    return jnp.concatenate(vals, axis=0)


def _count_ge(x, thr):
    return jnp.sum(jnp.where(x >= thr, 1.0, 0.0), axis=0, keepdims=True)


def _route_by_value(s1, s2, sel_a, sel_b, sel_at, fidx):
    v1 = _top_values(s1)
    v2 = _top_values(s2)
    cand = _sel_lhs(sel_a, v1) + _sel_lhs(sel_b, v2)
    cand = jnp.where(fidx < PAD_FLAT_INDEX, cand, NEG_INF)
    tops = _top_values(cand)
    zsum = jnp.sum(jnp.exp(tops - tops[0:1, :]), axis=0, keepdims=True)
    picked = cand >= tops[PEER_TOPK - 1:PEER_TOPK, :]
    cnt = _dot(sel_at, jnp.where(picked, 1.0, 0.0).astype(BF16))
    tm = s1.shape[1]
    th_a = jnp.full((PEER_TOPK, tm), jnp.inf, F32)
    for b in range(PEER_TOPK):
        th_a = jnp.where(cnt == float(b + 1), v2[b:b + 1, :], th_a)
    th = jnp.full(s1.shape, jnp.inf, F32)
    for a in range(PEER_TOPK):
        th = jnp.where(s1 == v1[a:a + 1, :], th_a[a:a + 1, :], th)
    in1 = s1 >= v1[PEER_TOPK - 1:PEER_TOPK, :]
    in2 = s2 >= v2[PEER_TOPK - 1:PEER_TOPK, :]
    e1 = jnp.where(in1, jnp.exp(s1 - v1[0:1, :]), 0.0) * (1.0 / zsum)
    e2 = jnp.where(in2, jnp.exp(s2 - v2[0:1, :]), 0.0)
    k = float(PEER_TOPK)
    clean = ((jnp.sum(jnp.where(in1, 1.0, 0.0), axis=0, keepdims=True) == k)
             & (jnp.sum(jnp.where(in2, 1.0, 0.0), axis=0, keepdims=True) == k)
             & (jnp.sum(jnp.where(picked, 1.0, 0.0), axis=0, keepdims=True) == k))
    tie = jnp.max(jnp.where(clean, 0.0, 1.0))
    return e1, th, e2, s2, tie


def _topk_columns(s, key_iota):
    rank = jnp.full(s.shape, NOT_RANKED, I32)
    vals = []
    x = s
    for a in range(PEER_TOPK):
        m = jnp.max(x, axis=0, keepdims=True)
        first = jnp.min(jnp.where(x == m, key_iota, PEER_NKEYS), axis=0, keepdims=True)
        hit = key_iota == first
        rank = jnp.where(hit, a, rank)
        x = jnp.where(hit, NEG_INF, x)
        vals.append(m)
    return jnp.concatenate(vals, axis=0), rank


def _route_by_rank(s1, s2, sel_a, sel_b, fidx):
    tm = s1.shape[1]
    key_iota = lax.broadcasted_iota(I32, (PEER_NKEYS, tm), 0)
    a_iota = lax.broadcasted_iota(I32, (PEER_TOPK, tm), 0)
    v1, r1 = _topk_columns(s1, key_iota)
    v2, r2 = _topk_columns(s2, key_iota)
    cand = _sel_lhs(sel_a, v1) + _sel_lhs(sel_b, v2)
    cand = jnp.where(fidx < PAD_FLAT_INDEX, cand, NEG_INF)
    top = None
    zsum = None
    cnt = jnp.zeros((PEER_TOPK, tm), F32)
    for k in range(PEER_TOPK):
        m = jnp.max(cand, axis=0, keepdims=True)
        f = jnp.min(jnp.where(cand == m, fidx, PAD_FLAT_INDEX), axis=0, keepdims=True)
        cand = jnp.where(fidx == f, NEG_INF, cand)
        if k == 0:
            top = m
            zsum = jnp.ones_like(m)
        else:
            zsum = zsum + jnp.exp(m - top)
        cnt = jnp.where(a_iota == (f >> 4), cnt + 1.0, cnt)
    th_a = jnp.where(cnt > 0.0, 1.0 - cnt, jnp.inf)
    th = jnp.full(s1.shape, jnp.inf, F32)
    for a in range(PEER_TOPK):
        th = jnp.where(r1 == a, th_a[a:a + 1, :], th)
    e1 = jnp.where(r1 < NOT_RANKED, jnp.exp(s1 - v1[0:1, :]), 0.0) * (1.0 / zsum)
    e2 = jnp.where(r2 < NOT_RANKED, jnp.exp(s2 - v2[0:1, :]), 0.0)
    kb = jnp.where(r2 < NOT_RANKED, -r2.astype(F32), NEG_INF)
    return e1, th, e2, kb


def _peer_route_kernel(h_ref, nw_ref, wqh_ref, wql_ref, kh_ref, kl_ref, sela_ref, selb_ref, selat_ref, fidx_ref,
                       xn_ref, e1_ref, th_ref, e2_ref, kb_ref, qt_ref):
    xn = _rms(h_ref[...], nw_ref[...])
    xn_hi = xn.astype(BF16)
    xn_lo = (xn - xn_hi.astype(F32)).astype(BF16)
    xn_ref[...] = xn_hi
    qt_ref[...] = _dot_nt(wqh_ref[...], xn_hi) + _dot_nt(wqh_ref[...], xn_lo) + _dot_nt(wql_ref[...], xn_hi)

    def one_head(hd, carry):
        scores = []
        for c in range(2):
            hc = 2 * hd + c
            qs = qt_ref[pl.ds(pl.multiple_of(hc * PEER_HALF, PEER_HALF), PEER_HALF), :]
            q_hi = qs.astype(BF16)
            q_lo = (qs - q_hi.astype(F32)).astype(BF16)
            scores.append(_dot(kh_ref[hc], q_hi) + _dot(kh_ref[hc], q_lo) + _dot(kl_ref[hc], q_hi))
        e1, th, e2, kb, tie = _route_by_value(scores[0], scores[1], sela_ref[...], selb_ref[...], selat_ref[...],
                                              fidx_ref[...])
        e1_ref[hd] = e1
        th_ref[hd] = th
        e2_ref[hd] = e2
        kb_ref[hd] = kb

        @pl.when(tie > 0.0)
        def _():
            e1x, thx, e2x, kbx = _route_by_rank(scores[0], scores[1], sela_ref[...], selb_ref[...], fidx_ref[...])
            e1_ref[hd] = e1x
            th_ref[hd] = thx
            e2_ref[hd] = e2x
            kb_ref[hd] = kbx

        return carry

    lax.fori_loop(0, PEER_HEADS, one_head, 0)


def _peer_route(h, p, tm):
    t = h.shape[0]
    tab = pl.BlockSpec((PEER_HEADS, PEER_NKEYS, tm), lambda i: (0, 0, i))
    tab_shape = jax.ShapeDtypeStruct((PEER_HEADS, PEER_NKEYS, t), F32)
    fidx = jnp.asarray(np.tile(np.array([a * PEER_TOPK + b for a, b in _STAIR]
                                        + [PAD_FLAT_INDEX] * (STAIR_ROWS - len(_STAIR)), np.int32)[:, None], (1, tm)))
    n_q = 2 * PEER_HEADS * PEER_HALF
    return pl.pallas_call(
        _peer_route_kernel,
        grid=(t // tm,),
        in_specs=[pl.BlockSpec((tm, D_MODEL), lambda i: (i, 0)), _full((1, D_MODEL)),
                  _full((n_q, D_MODEL)), _full((n_q, D_MODEL)),
                  _full((2 * PEER_HEADS, PEER_NKEYS, PEER_HALF)), _full((2 * PEER_HEADS, PEER_NKEYS, PEER_HALF)),
                  _full((STAIR_ROWS, PEER_TOPK)), _full((STAIR_ROWS, PEER_TOPK)), _full((PEER_TOPK, STAIR_ROWS)),
                  _full((STAIR_ROWS, tm))],
        out_specs=[pl.BlockSpec((tm, D_MODEL), lambda i: (i, 0)), tab, tab, tab, tab],
        out_shape=[jax.ShapeDtypeStruct((t, D_MODEL), BF16), tab_shape, tab_shape, tab_shape, tab_shape],
        scratch_shapes=[pltpu.VMEM((n_q, tm), F32)],
        compiler_params=_cparams("parallel"),
        name="peer_route",
    )(h, p["norm_ffn"], p["wq_hi"], p["wq_lo"], p["keys_hi"], p["keys_lo"], p["sel_a"], p["sel_b"], p["sel_at"],
      fidx)


EXPERT_BLOCK = 1024
ROWS_PER_BLOCK = EXPERT_BLOCK // PEER_NKEYS


def _gelu_tanh(x):
    return 0.5 * x * (1.0 + jnp.tanh(0.7978845608028654 * (x + 0.044715 * (x * x * x))))


def _peer_mix_kernel(h_ref, xn_ref, e1_ref, th_ref, e2_ref, kb_ref, u_ref, vt_ref, o_ref, wg_ref, acc_ref,
                     ht_ref):
    j = pl.program_id(1)

    @pl.when(j == 0)
    def _():
        acc_ref[...] = jnp.zeros(acc_ref.shape, F32)

    ht_ref[...] = _dot_nt(u_ref[...], xn_ref[...])
    tm = ht_ref.shape[1]
    for r in range(ROWS_PER_BLOCK):
        rows = slice(r * PEER_NKEYS, (r + 1) * PEER_NKEYS)
        for c in range(tm // LANES):
            cols = slice(c * LANES, (c + 1) * LANES)
            w = None
            for hd in range(PEER_HEADS):
                hit = kb_ref[hd, :, cols] >= th_ref[hd, r:r + 1, cols]
                term = jnp.where(hit, e1_ref[hd, r:r + 1, cols] * e2_ref[hd, :, cols], 0.0)
                w = term if w is None else w + term
            wg_ref[rows, cols] = (w * _gelu_tanh(ht_ref[rows, cols])).astype(BF16)
    acc_ref[...] += _dot(vt_ref[...], wg_ref[...])

    @pl.when(j == pl.num_programs(1) - 1)
    def _():
        o_ref[...] = h_ref[...] + acc_ref[...].T


def _peer_mix(h, xn, e1, th, e2, kb, u, vt, tm):
    t = h.shape[0]
    n_exp = u.shape[0]
    rowblk = pl.BlockSpec((PEER_HEADS, ROWS_PER_BLOCK, tm), lambda i, j: (0, j, i))
    fullblk = pl.BlockSpec((PEER_HEADS, PEER_NKEYS, tm), lambda i, j: (0, 0, i))
    return pl.pallas_call(
        _peer_mix_kernel,
        grid=(t // tm, n_exp // EXPERT_BLOCK),
        in_specs=[pl.BlockSpec((tm, D_MODEL), lambda i, j: (i, 0)), pl.BlockSpec((tm, D_MODEL), lambda i, j: (i, 0)),
                  rowblk, rowblk, fullblk, fullblk,
                  pl.BlockSpec((EXPERT_BLOCK, D_MODEL), lambda i, j: (j, 0)),
                  pl.BlockSpec((D_MODEL, EXPERT_BLOCK), lambda i, j: (0, j))],
        out_specs=pl.BlockSpec((tm, D_MODEL), lambda i, j: (i, 0)),
        out_shape=jax.ShapeDtypeStruct((t, D_MODEL), F32),
        scratch_shapes=[pltpu.VMEM((EXPERT_BLOCK, tm), BF16), pltpu.VMEM((D_MODEL, tm), F32),
                        pltpu.VMEM((EXPERT_BLOCK, tm), F32)],
        compiler_params=_cparams("parallel", "arbitrary"),
        name="peer_mix",
    )(h, xn, e1, th, e2, kb, u, vt)


def _ple_kernel(h_ref, p_ref, nw_ref, wg_ref, wp_ref, fnw_ref, o_ref, *, final):
    h = h_ref[...]
    gate = 1.0 / (1.0 + jnp.exp(-_dot(_rms(h, nw_ref[...]).astype(BF16), wg_ref[...])))
    out = h + gate * _dot(p_ref[...].astype(BF16), wp_ref[...])
    if final:
        out = _rms(out, fnw_ref[...])
    o_ref[...] = out


def _ple(h, pemb, p, fnw, final, tm):
    t = h.shape[0]
    pd = pemb.shape[1]
    return pl.pallas_call(
        functools.partial(_ple_kernel, final=final),
        grid=(t // tm,),
        in_specs=[pl.BlockSpec((tm, D_MODEL), lambda i: (i, 0)), pl.BlockSpec((tm, pd), lambda i: (i, 0)),
                  _full((1, D_MODEL)), _full((D_MODEL, D_MODEL)), _full((pd, D_MODEL)), _full((1, D_MODEL))],
        out_specs=pl.BlockSpec((tm, D_MODEL), lambda i: (i, 0)),
        out_shape=jax.ShapeDtypeStruct((t, D_MODEL), F32),
        compiler_params=_cparams("parallel"),
        name="ple",
    )(h, pemb, p["ple_norm"], p["w_ple_gate"], p["w_ple_proj"], fnw)


def _row(v, width=None):
    v = v.reshape(1, -1).astype(F32)
    if width is not None and v.shape[1] < width:
        v = jnp.pad(v, ((0, 0), (0, width - v.shape[1])))
    return v


def _prep_even(w, e):
    win = w["w_in_e"][e]
    c0, c1, c2 = SSD_WIDTH, SSD_WIDTH + CONV_CH, SSD_WIDTH + CONV_CH + SSD_HEADS
    w_cat = jnp.concatenate([win[:, :c0], win[:, c0:c1], win[:, c2:], win[:, c1:c2],
                             jnp.zeros((D_MODEL, LANES - SSD_HEADS), F32)], axis=1).astype(BF16)
    expand = np.zeros((LANES, SSD_WIDTH), np.float32)
    for hh in range(SSD_HEADS):
        expand[hh, hh * SSD_HEAD_DIM:(hh + 1) * SSD_HEAD_DIM] = 1.0
    return dict(
        w_in=w_cat, w_dt_t=win[:, c1:c2].T.astype(BF16),
        conv_w=w["conv_w"][e], conv_b=_row(w["conv_b"][e]),
        dt_bias=_row(w["dt_bias"][e], LANES),
        dt_bias_t=jnp.broadcast_to(w["dt_bias"][e][:, None], (SSD_HEADS, SSD_BLOCK)),
        a_log=_row(w["a_log"][e], LANES),
        a_log_t=jnp.broadcast_to(w["a_log"][e][:, None], (SSD_HEADS, SSD_BLOCK)),
        d_skip=_row(jnp.repeat(w["d_skip"][e], SSD_HEAD_DIM)), ssd_norm_w=_row(w["ssd_norm_w"][e]),
        expand=jnp.asarray(expand, BF16),
        pool_w=w["pool_w"][e].astype(BF16), pool_scale=_row(w["pool_scale"][e]),
        w_out=w["w_out_e"][e].astype(BF16))


def _prep_odd(w, o):
    half = QK_ROPE // 2
    win = w["w_in_o"][o]
    w_kpe = win[:, Q_LORA + KV_LORA:]
    w_kpe_sw = jnp.concatenate([-w_kpe[:, half:], w_kpe[:, :half]], axis=1)
    zpad = jnp.zeros((D_MODEL, LANES - QK_ROPE), F32)
    w_cat = jnp.concatenate([win[:, :Q_LORA + KV_LORA], w_kpe, zpad, w_kpe_sw, zpad], axis=1).astype(BF16)
    wuq = w["w_uq"][o].reshape(Q_LORA, MLA_HEADS, QK_NOPE + QK_ROPE)
    nope, pe = wuq[..., :QK_NOPE], wuq[..., QK_NOPE:]
    pe_sw = jnp.concatenate([-pe[..., half:], pe[..., :half]], axis=-1)
    z32 = jnp.zeros((Q_LORA, MLA_HEADS, HEAD_SLOT - QK_NOPE - QK_ROPE), F32)
    w_qa = jnp.concatenate([nope, pe, z32], axis=-1).reshape(Q_LORA, QK_ROWS).astype(BF16)
    w_qb = jnp.concatenate([jnp.zeros_like(nope), pe_sw, z32], axis=-1).reshape(Q_LORA, QK_ROWS).astype(BF16)
    wukv = w["w_ukv"][o].reshape(KV_LORA, MLA_HEADS, QK_NOPE + V_DIM)
    w_k = jnp.concatenate([wukv[..., :QK_NOPE], jnp.zeros((KV_LORA, MLA_HEADS, HEAD_SLOT - QK_NOPE), F32)],
                          axis=-1).reshape(KV_LORA, QK_ROWS).astype(BF16)
    w_vt = wukv[..., QK_NOPE:].reshape(KV_LORA, MLA_HEADS * V_DIM).T.astype(BF16)
    r128 = np.zeros((half, LANES), np.float32)
    rq = np.zeros((half, QK_ROWS), np.float32)
    rq_mask = np.zeros((1, QK_ROWS), np.float32)
    r_k = np.zeros((QK_ROPE, QK_ROWS), np.float32)
    for i in range(half):
        r128[i, i] = r128[i, half + i] = 1.0
    for hh in range(MLA_HEADS):
        base = hh * HEAD_SLOT
        rq_mask[0, base:base + QK_NOPE] = 1.0
        for i in range(half):
            rq[i, base + QK_NOPE + i] = rq[i, base + QK_NOPE + half + i] = 1.0
        for i in range(QK_ROPE):
            r_k[i, base + QK_NOPE + i] = 1.0
    return dict(
        w_in=w_cat, q_norm=_row(w["q_norm"][o]), kv_norm=_row(w["kv_norm"][o]), w_qa=w_qa, w_qb=w_qb,
        r128=jnp.asarray(r128, BF16), rq=jnp.asarray(rq, BF16), rq_mask=jnp.asarray(rq_mask),
        w_k=w_k, r_k=jnp.asarray(r_k, BF16), w_vt=w_vt, w_out=w["w_out_o"][o].astype(BF16))


def _hi_lo(x):
    hi = x.astype(BF16)
    return hi, (x - hi.astype(F32)).astype(BF16)


def _prep_common(w, i):
    wq_hi, wq_lo = _hi_lo(w["peer_wq"][i].T)
    keys_hi, keys_lo = _hi_lo(w["peer_keys"][i].reshape(2 * PEER_HEADS, PEER_NKEYS, PEER_HALF))
    sel_a = np.zeros((STAIR_ROWS, PEER_TOPK), np.float32)
    sel_b = np.zeros((STAIR_ROWS, PEER_TOPK), np.float32)
    for row_i, (a, b) in enumerate(_STAIR):
        sel_a[row_i, a] = 1.0
        sel_b[row_i, b] = 1.0
    return dict(
        norm_mix=_row(w["norm_mix"][i]), norm_ffn=_row(w["norm_ffn"][i]), ple_norm=_row(w["ple_norm"][i]),
        wq_hi=wq_hi, wq_lo=wq_lo, keys_hi=keys_hi, keys_lo=keys_lo,
        sel_a=jnp.asarray(sel_a, BF16), sel_b=jnp.asarray(sel_b, BF16), sel_at=jnp.asarray(sel_a.T, BF16),
        peer_u=w["peer_u"][i].astype(BF16), peer_vt=w["peer_v"][i].T.astype(BF16),
        w_ple_gate=w["w_ple_gate"][i].astype(BF16), w_ple_proj=w["w_ple_proj"][i].astype(BF16))


def _pad_axis(x, axis, size):
    if x.shape[axis] == size:
        return x
    pads = [(0, 0)] * x.ndim
    pads[axis] = (0, size - x.shape[axis])
    return jnp.pad(x, pads)


def _round_up(n, m):
    return -(-n // m) * m


def _even_layer(h, conv_st, ssm_st, pool_st, pos0, p, cfg):
    b, l, _ = h.shape
    z, xbc, u, dt, dtt = _even_in(h.reshape(b * l, D_MODEL), p["norm_mix"], p["w_in"], p["w_dt_t"], cfg["tm"])
    z, xbc, u, dt = (a.reshape(b, l, -1) for a in (z, xbc, u, dt))
    dtt = dtt.reshape(SSD_HEADS, b, l).transpose(1, 0, 2)
    lp = _round_up(l, SSD_BLOCK)
    r = SSD_HEADS // SSD_GROUPS
    s0 = ssm_st.astype(F32).reshape(b, SSD_GROUPS, r, SSD_HEAD_DIM, SSD_STATE).transpose(0, 1, 4, 2, 3)
    s0 = s0.reshape(b, SSD_GROUPS, SSD_STATE, r * SSD_HEAD_DIM)
    hist = jnp.pad(conv_st.astype(F32), ((0, 0), (8 - (CONV_WIDTH - 1), 0), (0, 0)))
    y, s_fin = _ssd(_pad_axis(xbc, 1, lp), _pad_axis(z, 1, lp), _pad_axis(dt, 1, lp), _pad_axis(dtt, 2, lp),
                    hist, s0, p, l)
    new_ssm = s_fin.reshape(b, SSD_GROUPS, SSD_STATE, r, SSD_HEAD_DIM).transpose(0, 1, 3, 4, 2)
    new_ssm = new_ssm.reshape(b, SSD_HEADS, SSD_HEAD_DIM, SSD_STATE)
    new_conv = jnp.concatenate([conv_st.astype(F32), xbc], axis=1)[:, -(CONV_WIDTH - 1):]
    new_pool = jnp.concatenate([pool_st.astype(F32), u], axis=1)[:, -POOL_HIST:]
    phist = jnp.pad(pool_st.astype(F32), ((0, 0), (16 - POOL_HIST, 0), (0, 0)))
    h_new = _pool_out(u, phist, y[:, :l], h, p, pos0, cfg["tm_seq"])
    return h_new, new_conv, new_ssm, new_pool


def _odd_layer(h, ckv_hist, kpe_hist, pos0, p, cfg):
    b, l, _ = h.shape
    half = QK_ROPE // 2
    pos = (pos0 + jnp.arange(l)).astype(F32)
    inv = ROPE_THETA ** (-jnp.arange(half, dtype=F32) / half)
    ang = pos[:, None] * inv[None, :]
    cos = jnp.tile(jnp.cos(ang), (b, 1))
    sin = jnp.tile(jnp.sin(ang), (b, 1))
    hf = h.reshape(b * l, D_MODEL)
    ckv, kpe, qt = _odd_in(hf, cos, sin, p, cfg["tm"])
    ckv = ckv.reshape(b, l, KV_LORA)
    kpe = kpe.reshape(b, l, QK_ROPE)
    ckv_all = jnp.concatenate([ckv_hist.astype(F32), ckv], axis=1)
    kpe_all = jnp.concatenate([kpe_hist.astype(F32), kpe], axis=1)
    n_keys = ckv_all.shape[1]
    causal = ckv_hist.shape[1] == 0
    if causal:
        tq, tk, nk_pad, lq = cfg["tq"], cfg["tk"], n_keys, l
    else:
        nk_pad = _round_up(n_keys, ATTN_SUB)
        tq, tk, lq = LANES, nk_pad, _round_up(l, LANES)
    k, vt = _kv_expand(_pad_axis(ckv_all, 1, nk_pad), _pad_axis(kpe_all, 1, nk_pad), p, cfg["tm_kv"])
    qt = _pad_axis(qt.reshape(QK_ROWS, b, l).transpose(1, 0, 2), 2, lq)
    ot = _attention(qt, k, vt, tq, tk, causal, None if causal else n_keys)
    ot = ot[:, :, :l].transpose(1, 0, 2).reshape(MLA_HEADS * V_DIM, b * l)
    h_new = _attn_out(ot, hf, p["w_out"], cfg["tm"]).reshape(b, l, D_MODEL)
    return h_new, ckv, kpe


def _prep_layers(w):
    layers = []
    for i in range(DEPTH):
        pc = _prep_common(w, i)
        layers.append(dict(pc, **(_prep_even(w, i // 2) if i % 2 == 0 else _prep_odd(w, i // 2))))
    return layers


def _trunk(x, pemb, conv_st, ssm_st, pool_st, ckv_h, kpe_h, pos0, layers, fnw, cfg):
    b, l, _ = x.shape
    h = x.astype(F32)
    convs, ssms, pools, ckvs, kpes = [], [], [], [], []
    for i in range(DEPTH):
        pc = layers[i]
        if i % 2 == 0:
            e = i // 2
            h, c_new, s_new, p_new = _even_layer(h, conv_st[e], ssm_st[e], pool_st[e], pos0, pc, cfg)
            convs.append(c_new)
            ssms.append(s_new)
            pools.append(p_new)
        else:
            o = i // 2
            h, ckv_new, kpe_new = _odd_layer(h, ckv_h[o], kpe_h[o], pos0, pc, cfg)
            ckvs.append(ckv_new)
            kpes.append(kpe_new)
        hf = h.reshape(b * l, D_MODEL)
        xn, e1, th, e2, kb = _peer_route(hf, pc, cfg["tm_route"])
        hf = _peer_mix(hf, xn, e1, th, e2, kb, pc["peer_u"], pc["peer_vt"], cfg["tm_mix"])
        hf = _ple(hf, pemb[i].reshape(b * l, -1), pc, fnw, i == DEPTH - 1, cfg["tm"])
        h = hf.reshape(b, l, D_MODEL)
    return h, jnp.stack(convs), jnp.stack(ssms), jnp.stack(pools), jnp.stack(ckvs), jnp.stack(kpes)


def _tile(n, pref):
    t = min(pref, n)
    while n % t:
        t -= 8
    return t


def _config(b, l):
    t = b * l
    return dict(tm=_tile(t, 512), tm_seq=_tile(l, 512), tm_route=_tile(t, 256), tm_mix=_tile(t, 512),
                tq=_tile(l, 512), tk=_tile(l, 512), tm_kv=LANES)


def kernel(x_prompt, x_sample, state_conv, state_ssm, state_pool, cache_ckv, cache_kpe, p_prompt, p_sample,
           norm_mix, norm_ffn, ple_norm, final_norm, w_in_e, conv_w, conv_b, dt_bias, a_log, d_skip, ssd_norm_w,
           pool_w, pool_scale, w_out_e, w_in_o, q_norm, kv_norm, w_uq, w_ukv, w_out_o, peer_wq, peer_keys, peer_u,
           peer_v, w_ple_proj, w_ple_gate):
    w = dict(norm_mix=norm_mix, norm_ffn=norm_ffn, ple_norm=ple_norm, final_norm=final_norm, w_in_e=w_in_e,
             conv_w=conv_w, conv_b=conv_b, dt_bias=dt_bias, a_log=a_log, d_skip=d_skip, ssd_norm_w=ssd_norm_w,
             pool_w=pool_w, pool_scale=pool_scale, w_out_e=w_out_e, w_in_o=w_in_o, q_norm=q_norm, kv_norm=kv_norm,
             w_uq=w_uq, w_ukv=w_ukv, w_out_o=w_out_o, peer_wq=peer_wq, peer_keys=peer_keys, peer_u=peer_u,
             peer_v=peer_v, w_ple_proj=w_ple_proj, w_ple_gate=w_ple_gate)
    n_even, n_odd = (DEPTH + 1) // 2, DEPTH // 2
    b0, l0, _ = x_prompt.shape
    conv0 = jnp.zeros((n_even, b0, CONV_WIDTH - 1, CONV_CH), F32)
    ssm0 = jnp.zeros((n_even, b0, SSD_HEADS, SSD_HEAD_DIM, SSD_STATE), F32)
    pool0 = jnp.zeros((n_even, b0, POOL_HIST, SSD_WIDTH), F32)
    ckv0 = jnp.zeros((n_odd, b0, 0, KV_LORA), F32)
    kpe0 = jnp.zeros((n_odd, b0, 0, QK_ROPE), F32)
    layers = _prep_layers(w)
    fnw = _row(final_norm)
    outs_p = _trunk(x_prompt, p_prompt, conv0, ssm0, pool0, ckv0, kpe0, 0, layers, fnw, _config(b0, l0))
    b1, l1, _ = x_sample.shape
    pos0 = cache_ckv.shape[2]
    outs_s = _trunk(x_sample, p_sample, state_conv, state_ssm, state_pool, cache_ckv, cache_kpe, pos0, layers, fnw,
                    _config(b1, l1))
    return (outs_p[0], outs_s[0]) + outs_p[1:] + outs_s[1:]
```

```python
import functools

import numpy as np
import jax
import jax.numpy as jnp
from jax import lax
from jax.experimental import pallas as pl
from jax.experimental.pallas import tpu as pltpu

F32 = jnp.float32
BF16 = jnp.bfloat16
I32 = jnp.int32

D_MODEL = 1024
DEPTH = 4
CHUNK = 64
EPS = 1e-6
SSD_HEAD_DIM = 64
SSD_HEADS = 16
SSD_WIDTH = 1024
SSD_GROUPS = 2
SSD_STATE = 128
CONV_WIDTH = 4
CONV_CH = 1536
POOL_WINDOWS = (2, 4, 8, 16)
POOL_GROUP_DIM = 256
POOL_HIST = 15
QK_NOPE = 64
QK_ROPE = 32
V_DIM = 64
MLA_HEADS = 16
Q_LORA = 256
KV_LORA = 256
ROPE_THETA = 10000.0
PEER_HEADS = 8
PEER_NKEYS = 128
PEER_HALF = 128
PEER_TOPK = 16

LANES = 128
BF16_ROWS = 16
VMEM_LIMIT_BYTES = 56 * 1024 * 1024
SSD_BLOCK = 128
HEAD_SLOT = 128

NEG_INF = float("-inf")


def _cparams(*sem):
    return pltpu.CompilerParams(dimension_semantics=sem, vmem_limit_bytes=VMEM_LIMIT_BYTES)


def _dot(a, b):
    return jnp.dot(a, b, preferred_element_type=F32)


def _dot_nt(a, b):
    return lax.dot_general(a, b, (((1,), (1,)), ((), ())), preferred_element_type=F32)


def _split3(x):
    hi = x.astype(BF16)
    r = x - hi.astype(F32)
    mid = r.astype(BF16)
    lo = (r - mid.astype(F32)).astype(BF16)
    return hi, mid, lo


def _sel_rhs(x, sel):
    hi, mid, lo = _split3(x)
    return _dot(hi, sel) + _dot(mid, sel) + _dot(lo, sel)


def _sel_lhs(sel, x):
    hi, mid, lo = _split3(x)
    return _dot(sel, hi) + _dot(sel, mid) + _dot(sel, lo)


def _rms(x, w):
    return x * lax.rsqrt(jnp.mean(x * x, axis=-1, keepdims=True) + EPS) * w


def _silu(x):
    return x * (1.0 / (1.0 + jnp.exp(-x)))


def _softplus(x):
    return jnp.maximum(x, 0.0) + jnp.log(1.0 + jnp.exp(-jnp.abs(x)))


def _full(shape):
    nd = len(shape)
    return pl.BlockSpec(shape, lambda *_: (0,) * nd)


EVEN_COLS = SSD_WIDTH + CONV_CH + SSD_WIDTH + LANES


def _even_in_kernel(h_ref, nw_ref, w_ref, wdt_ref, z_ref, xbc_ref, u_ref, dt_ref, dtt_ref):
    xn = _rms(h_ref[...], nw_ref[...]).astype(BF16)
    proj = _dot(xn, w_ref[...])
    z_ref[...] = proj[:, :SSD_WIDTH]
    xbc_ref[...] = proj[:, SSD_WIDTH:SSD_WIDTH + CONV_CH]
    u_ref[...] = proj[:, SSD_WIDTH + CONV_CH:2 * SSD_WIDTH + CONV_CH]
    dt_ref[...] = proj[:, 2 * SSD_WIDTH + CONV_CH:]
    dtt_ref[...] = _dot_nt(wdt_ref[...], xn)


def _even_in(h, nw, w, wdt, tm):
    t = h.shape[0]
    row = lambda n: pl.BlockSpec((tm, n), lambda i: (i, 0))
    return pl.pallas_call(
        _even_in_kernel,
        grid=(t // tm,),
        in_specs=[row(D_MODEL), _full((1, D_MODEL)), _full((D_MODEL, EVEN_COLS)), _full((SSD_HEADS, D_MODEL))],
        out_specs=[row(SSD_WIDTH), row(CONV_CH), row(SSD_WIDTH), row(LANES),
                   pl.BlockSpec((SSD_HEADS, tm), lambda i: (0, i))],
        out_shape=[jax.ShapeDtypeStruct((t, SSD_WIDTH), F32), jax.ShapeDtypeStruct((t, CONV_CH), F32),
                   jax.ShapeDtypeStruct((t, SSD_WIDTH), F32), jax.ShapeDtypeStruct((t, LANES), F32),
                   jax.ShapeDtypeStruct((SSD_HEADS, t), F32)],
        compiler_params=_cparams("parallel"),
        name="even_in",
    )(h, nw, w, wdt)


def _ssd_kernel(xbc_ref, z_ref, dt_ref, dtt_ref, hist_ref, s0_ref, cw_ref, cb_ref, bias_ref, biast_ref,
                a_ref, at_ref, dskip_ref, nw_ref, expand_ref, y_ref, sfin_ref, ext_ref, st_ref, *, valid_len):
    q = SSD_BLOCK
    c = pl.program_id(1)

    @pl.when(c == 0)
    def _():
        ext_ref[0:8, :] = hist_ref[0]
        st_ref[...] = s0_ref[0]

    ext_ref[8:8 + q, :] = xbc_ref[0]
    conv = cb_ref[...] + cw_ref[0:1, :] * ext_ref[5:5 + q, :]
    for j in range(1, CONV_WIDTH):
        conv = conv + cw_ref[j:j + 1, :] * ext_ref[5 + j:5 + j + q, :]
    ext_ref[0:8, :] = ext_ref[q:q + 8, :]
    xbc = _silu(conv)
    xs = xbc[:, :SSD_WIDTH]
    xs_b = xs.astype(BF16)

    row_i = lax.broadcasted_iota(I32, (q, q), 0)
    col_i = lax.broadcasted_iota(I32, (q, q), 1)
    tril = (row_i >= col_i).astype(BF16)
    triu = (row_i <= col_i).astype(BF16)
    ones = jnp.ones((q, q), BF16)
    tok_ok = (c * q + lax.broadcasted_iota(I32, (q, LANES), 0)) < valid_len
    dt_tok = jnp.where(tok_ok, _softplus(dt_ref[0] + bias_ref[...]), 0.0)
    cs_tok = _sel_lhs(tril, dt_tok * -jnp.exp(a_ref[...]))
    lane_ok = (c * q + lax.broadcasted_iota(I32, (SSD_HEADS, q), 1)) < valid_len
    dt_t = jnp.where(lane_ok, _softplus(dtt_ref[0] + biast_ref[...]), 0.0)
    da_t = dt_t * -jnp.exp(at_ref[...])
    cs_t = _sel_rhs(da_t, triu)
    tot_t = _sel_rhs(da_t, ones)
    w_t = dt_t * jnp.exp(tot_t - cs_t)
    cs_exp = _sel_rhs(cs_tok, expand_ref[...])
    ecs = jnp.exp(cs_exp)
    dchunk = ecs[q - 1:q, :]

    lane_lo = lax.broadcasted_iota(I32, (q, LANES), 1) < SSD_HEAD_DIM
    causal = row_i >= col_i
    r = SSD_HEADS // SSD_GROUPS
    gw = r * SSD_HEAD_DIM
    y_parts = []
    for g in range(SSD_GROUPS):
        bm = xbc[:, SSD_WIDTH + g * SSD_STATE:SSD_WIDTH + (g + 1) * SSD_STATE]
        cm = xbc[:, SSD_WIDTH + (SSD_GROUPS + g) * SSD_STATE:SSD_WIDTH + (SSD_GROUPS + g + 1) * SSD_STATE]
        cm_b = cm.astype(BF16)
        cb = _dot_nt(cm_b, bm.astype(BF16))
        bm_t = bm.T
        st_g = st_ref[g]
        y_off = _dot(cm_b, st_g.astype(BF16)) * ecs[:, g * gw:(g + 1) * gw]
        yd, up = [], []
        for pr in range(r // 2):
            res_y, res_u = [], []
            xs_pair = xs_b[:, g * gw + pr * LANES:g * gw + (pr + 1) * LANES]
            for k in range(2):
                hh = g * r + 2 * pr + k
                seg = cs_tok[:, hh:hh + 1] - cs_t[hh:hh + 1, :]
                lmat = jnp.where(causal, jnp.exp(seg), 0.0)
                m = (cb * lmat * dt_t[hh:hh + 1, :]).astype(BF16)
                res_y.append(_dot(m, xs_pair))
                res_u.append(_dot((bm_t * w_t[hh:hh + 1, :]).astype(BF16), xs_pair))
            yd.append(jnp.where(lane_lo, res_y[0], res_y[1]))
            up.append(jnp.where(lane_lo, res_u[0], res_u[1]))
        y_parts.append(jnp.concatenate(yd, axis=1) + y_off)
        st_ref[g] = dchunk[:, g * gw:(g + 1) * gw] * st_g + jnp.concatenate(up, axis=1)

    y = jnp.concatenate(y_parts, axis=1) + dskip_ref[...] * xs
    y = y * _silu(z_ref[0])
    outs = []
    for g in range(SSD_GROUPS):
        yg = y[:, g * gw:(g + 1) * gw]
        outs.append(yg * lax.rsqrt(jnp.mean(yg * yg, axis=-1, keepdims=True) + EPS))
    y_ref[0] = (jnp.concatenate(outs, axis=1) * nw_ref[...]).astype(y_ref.dtype)

    @pl.when(c == pl.num_programs(1) - 1)
    def _():
        sfin_ref[0] = st_ref[...]


def _ssd(xbc, z, dt, dtt, hist, s0, p, valid_len):
    b, l, _ = xbc.shape
    q = SSD_BLOCK
    n_state = SSD_GROUPS * SSD_STATE
    gw = SSD_WIDTH // SSD_GROUPS
    blk = lambda n: pl.BlockSpec((1, q, n), lambda i, c: (i, c, 0))
    per_b = lambda s: pl.BlockSpec((1,) + s, lambda i, c: (i,) + (0,) * len(s))
    return pl.pallas_call(
        functools.partial(_ssd_kernel, valid_len=valid_len),
        grid=(b, l // q),
        in_specs=[blk(CONV_CH), blk(SSD_WIDTH), blk(LANES),
                  pl.BlockSpec((1, SSD_HEADS, q), lambda i, c: (i, 0, c)),
                  per_b((8, CONV_CH)), per_b((SSD_GROUPS, SSD_STATE, gw)),
                  _full((CONV_WIDTH, CONV_CH)), _full((1, CONV_CH)), _full((1, LANES)), _full((SSD_HEADS, q)),
                  _full((1, LANES)), _full((SSD_HEADS, q)), _full((1, SSD_WIDTH)), _full((1, SSD_WIDTH)),
                  _full((LANES, SSD_WIDTH))],
        out_specs=[blk(SSD_WIDTH), per_b((SSD_GROUPS, SSD_STATE, gw))],
        out_shape=[jax.ShapeDtypeStruct((b, l, SSD_WIDTH), BF16),
                   jax.ShapeDtypeStruct((b, SSD_GROUPS, SSD_STATE, gw), F32)],
        scratch_shapes=[pltpu.VMEM((q + 8, CONV_CH), F32), pltpu.VMEM((SSD_GROUPS, SSD_STATE, gw), F32)],
        compiler_params=_cparams("parallel", "arbitrary"),
        name="ssd_scan",
    )(xbc, z, dt, dtt, hist, s0, p["conv_w"], p["conv_b"], p["dt_bias"], p["dt_bias_t"], p["a_log"], p["a_log_t"],
      p["d_skip"], p["ssd_norm_w"], p["expand"])


def _pool_out_kernel(u_ref, hist_ref, y_ref, h_ref, pw_ref, ps_ref, wo_ref, o_ref, ext_ref, *, pos0, tm):
    i = pl.program_id(1)

    @pl.when(i == 0)
    def _():
        ext_ref[0:16, :] = hist_ref[0]

    ext_ref[16:16 + tm, :] = u_ref[0]
    pos = pos0 + i * tm + lax.broadcasted_iota(I32, (tm, POOL_GROUP_DIM), 0)
    acc = h_ref[0] + _dot(y_ref[0], wo_ref[0:SSD_WIDTH, :])
    yps = []
    for gi, wsz in enumerate(POOL_WINDOWS):
        c0 = gi * POOL_GROUP_DIM
        cur = ext_ref[16:16 + tm, c0:c0 + POOL_GROUP_DIM]
        tot = cur
        for j in range(1, wsz):
            tot = tot + ext_ref[16 - j:16 - j + tm, c0:c0 + POOL_GROUP_DIM]
        cnt = jnp.minimum(pos + 1, wsz).astype(F32)
        pooled = tot / cnt - cur
        yps.append(_dot(pooled.astype(BF16), pw_ref[gi]))
    yp = (jnp.concatenate(yps, axis=1) * ps_ref[...]).astype(BF16)
    o_ref[0] = acc + _dot(yp, wo_ref[SSD_WIDTH:, :])
    ext_ref[0:16, :] = ext_ref[tm:tm + 16, :]


def _pool_out(u, hist, y, h, p, pos0, tm):
    b, l, _ = u.shape
    blk = lambda n: pl.BlockSpec((1, tm, n), lambda bi, i: (bi, i, 0))
    return pl.pallas_call(
        functools.partial(_pool_out_kernel, pos0=pos0, tm=tm),
        grid=(b, l // tm),
        in_specs=[blk(SSD_WIDTH), pl.BlockSpec((1, 16, SSD_WIDTH), lambda bi, i: (bi, 0, 0)), blk(SSD_WIDTH),
                  blk(D_MODEL), _full((len(POOL_WINDOWS), POOL_GROUP_DIM, POOL_GROUP_DIM)), _full((1, SSD_WIDTH)),
                  _full((2 * SSD_WIDTH, D_MODEL))],
        out_specs=blk(D_MODEL),
        out_shape=jax.ShapeDtypeStruct((b, l, D_MODEL), F32),
        scratch_shapes=[pltpu.VMEM((tm + 16, SSD_WIDTH), F32)],
        compiler_params=_cparams("parallel", "arbitrary"),
        name="pool_out",
    )(u, hist, y, h, p["pool_w"], p["pool_scale"], p["w_out"])


ODD_COLS = Q_LORA + KV_LORA + 2 * LANES
QK_ROWS = MLA_HEADS * HEAD_SLOT


def _odd_in_kernel(h_ref, nw_ref, w_ref, qn_ref, kvn_ref, wa_ref, wb_ref, cos_ref, sin_ref, r128_ref, rq_ref,
                   rqm_ref, ckv_ref, kpe_ref, qt_ref, *, scale):
    xn = _rms(h_ref[...], nw_ref[...]).astype(BF16)
    proj = _dot(xn, w_ref[...])
    cq = _rms(proj[:, :Q_LORA], qn_ref[...]).astype(BF16)
    ckv_ref[...] = _rms(proj[:, Q_LORA:Q_LORA + KV_LORA], kvn_ref[...])
    cos = cos_ref[...]
    sin = sin_ref[...]
    kpe = proj[:, Q_LORA + KV_LORA:Q_LORA + KV_LORA + LANES]
    kpe_sw = proj[:, Q_LORA + KV_LORA + LANES:]
    kpe_rot = kpe * _sel_rhs(cos, r128_ref[...]) + kpe_sw * _sel_rhs(sin, r128_ref[...])
    kpe_ref[...] = kpe_rot[:, :QK_ROPE]
    cos_q = (_sel_rhs(cos, rq_ref[...]) + rqm_ref[...]) * scale
    sin_q = _sel_rhs(sin, rq_ref[...]) * scale
    q = _dot(cq, wa_ref[...]) * cos_q + _dot(cq, wb_ref[...]) * sin_q
    qt_ref[...] = q.T.astype(qt_ref.dtype)


def _odd_in(h, cos, sin, p, tm):
    t = h.shape[0]
    row = lambda n: pl.BlockSpec((tm, n), lambda i: (i, 0))
    half = QK_ROPE // 2
    scale = float((QK_NOPE + QK_ROPE) ** -0.5 * np.log2(np.e))
    return pl.pallas_call(
        functools.partial(_odd_in_kernel, scale=scale),
        grid=(t // tm,),
        in_specs=[row(D_MODEL), _full((1, D_MODEL)), _full((D_MODEL, ODD_COLS)), _full((1, Q_LORA)),
                  _full((1, KV_LORA)), _full((Q_LORA, QK_ROWS)), _full((Q_LORA, QK_ROWS)), row(half), row(half),
                  _full((half, LANES)), _full((half, QK_ROWS)), _full((1, QK_ROWS))],
        out_specs=[row(KV_LORA), row(QK_ROPE), pl.BlockSpec((QK_ROWS, tm), lambda i: (0, i))],
        out_shape=[jax.ShapeDtypeStruct((t, KV_LORA), F32), jax.ShapeDtypeStruct((t, QK_ROPE), F32),
                   jax.ShapeDtypeStruct((QK_ROWS, t), BF16)],
        compiler_params=_cparams("parallel"),
        name="odd_in",
    )(h, p["norm_mix"], p["w_in"], p["q_norm"], p["kv_norm"], p["w_qa"], p["w_qb"], cos, sin, p["r128"], p["rq"],
      p["rq_mask"])


def _kv_expand_kernel(ckv_ref, kpe_ref, wk_ref, rk_ref, wvt_ref, k_ref, vt_ref):
    ckv = ckv_ref[0].astype(BF16)
    k_ref[0] = (_dot(ckv, wk_ref[...]) + _dot(kpe_ref[0].astype(BF16), rk_ref[...])).astype(k_ref.dtype)
    vt_ref[0] = _dot_nt(wvt_ref[...], ckv).astype(vt_ref.dtype)


def _kv_expand(ckv, kpe, p, tm):
    b, t, _ = ckv.shape
    vrows = MLA_HEADS * V_DIM
    return pl.pallas_call(
        _kv_expand_kernel,
        grid=(b, t // tm),
        in_specs=[pl.BlockSpec((1, tm, KV_LORA), lambda bi, i: (bi, i, 0)),
                  pl.BlockSpec((1, tm, QK_ROPE), lambda bi, i: (bi, i, 0)),
                  _full((KV_LORA, QK_ROWS)), _full((QK_ROPE, QK_ROWS)), _full((vrows, KV_LORA))],
        out_specs=[pl.BlockSpec((1, tm, QK_ROWS), lambda bi, i: (bi, i, 0)),
                   pl.BlockSpec((1, vrows, tm), lambda bi, i: (bi, 0, i))],
        out_shape=[jax.ShapeDtypeStruct((b, t, QK_ROWS), BF16), jax.ShapeDtypeStruct((b, vrows, t), BF16)],
        compiler_params=_cparams("parallel", "parallel"),
        name="kv_expand",
    )(ckv, kpe, p["w_k"], p["r_k"], p["w_vt"])


ATTN_HEADS_PER_STEP = 8
ATTN_SUB = 256
ATTN_LOOKAHEAD = 5


def _attn_kernel(qi_ref, ki_ref, qt_ref, k_ref, vt_ref, o_ref, m_ref, l_ref, acc_ref, *, tq, tk, causal,
                 kv_valid):
    s_idx = pl.program_id(2)
    qi = qi_ref[s_idx]
    ki = ki_ref[s_idx]
    nh = ATTN_HEADS_PER_STEP

    @pl.when(ki == 0)
    def _():
        m_ref[...] = jnp.full(m_ref.shape, NEG_INF, F32)
        l_ref[...] = jnp.zeros(l_ref.shape, F32)
        acc_ref[...] = jnp.zeros(acc_ref.shape, F32)

    sk, sq = min(ATTN_SUB, tk), min(ATTN_SUB, tq)

    def step(diagonal):
        items = []
        for ks in range(tk // sk):
            for hh in range(nh):
                for qs in range(tq // sq):
                    if causal and diagonal and ks * sk >= (qs + 1) * sq:
                        continue
                    if not causal and kv_valid is not None and ks * sk >= kv_valid:
                        continue
                    items.append((ks, hh, qs))

        def needs_mask(ks, qs):
            if causal:
                return diagonal and (ks + 1) * sk > qs * sq
            return kv_valid is not None and (ks + 1) * sk > kv_valid

        def scores(ks, hh, qs):
            return _dot(k_ref[0, ks * sk:(ks + 1) * sk, hh * HEAD_SLOT:(hh + 1) * HEAD_SLOT],
                        qt_ref[0, hh * HEAD_SLOT:(hh + 1) * HEAD_SLOT, qs * sq:(qs + 1) * sq])

        state = {}
        for hh in range(nh):
            for qs in range(tq // sq):
                cols = slice(qs * sq, (qs + 1) * sq)
                state[hh, qs] = (m_ref[hh:hh + 1, cols], l_ref[hh:hh + 1, cols],
                                 acc_ref[hh * V_DIM:(hh + 1) * V_DIM, cols])
        pending = [scores(*it) for it in items[:ATTN_LOOKAHEAD]]
        for idx, (ks, hh, qs) in enumerate(items):
            st = pending.pop(0)
            if idx + ATTN_LOOKAHEAD < len(items):
                pending.append(scores(*items[idx + ATTN_LOOKAHEAD]))
            if needs_mask(ks, qs):
                kpos = ks * sk + lax.broadcasted_iota(I32, (sk, sq), 0)
                if causal:
                    qpos = qs * sq + lax.broadcasted_iota(I32, (sk, sq), 1)
                    ok = (kpos // CHUNK) <= (qpos // CHUNK)
                else:
                    ok = kpos < kv_valid
                st = jnp.where(ok, st, NEG_INF)
            m_prev, l_prev, acc_prev = state[hh, qs]
            m_new = jnp.maximum(m_prev, jnp.max(st, axis=0, keepdims=True))
            alpha = jnp.exp2(m_prev - m_new)
            pt = jnp.exp2(st - m_new)
            l_new = alpha * l_prev + jnp.sum(pt, axis=0, keepdims=True)
            pv = _dot(vt_ref[0, hh * V_DIM:(hh + 1) * V_DIM, ks * sk:(ks + 1) * sk], pt.astype(BF16))
            state[hh, qs] = (m_new, l_new, alpha * acc_prev + pv)
        for (hh, qs), (m_new, l_new, acc_new) in state.items():
            cols = slice(qs * sq, (qs + 1) * sq)
            m_ref[hh:hh + 1, cols] = m_new
            l_ref[hh:hh + 1, cols] = l_new
            acc_ref[hh * V_DIM:(hh + 1) * V_DIM, cols] = acc_new

    def finish():
        for hh in range(nh):
            vrows = slice(hh * V_DIM, (hh + 1) * V_DIM)
            o_ref[0, vrows, :] = (acc_ref[vrows, :] / l_ref[hh:hh + 1, :]).astype(o_ref.dtype)

    if causal:
        last = ((qi + 1) * tq - 1) // tk

        @pl.when(ki < last)
        def _():
            step(False)

        @pl.when(ki == last)
        def _():
            step(True)
            finish()
    else:
        step(False)
        finish()


def _attention(qt, k, vt, tq, tk, causal, kv_valid):
    b, _, t_q = qt.shape
    t_k = k.shape[1]
    nq, nk = t_q // tq, t_k // tk
    nh = ATTN_HEADS_PER_STEP
    if causal:
        assert tk == tq
        pairs = [(a, c) for a in range(nq) for c in range(((a + 1) * tq - 1) // tk + 1)]
    else:
        assert nk == 1
        pairs = [(a, 0) for a in range(nq)]
    qi = jnp.asarray(np.array([a for a, _ in pairs], np.int32))
    ki = jnp.asarray(np.array([c for _, c in pairs], np.int32))
    grid_spec = pltpu.PrefetchScalarGridSpec(
        num_scalar_prefetch=2,
        grid=(b, MLA_HEADS // nh, len(pairs)),
        in_specs=[pl.BlockSpec((1, nh * HEAD_SLOT, tq), lambda bi, hi, s, qi_r, ki_r: (bi, hi, qi_r[s])),
                  pl.BlockSpec((1, tk, nh * HEAD_SLOT), lambda bi, hi, s, qi_r, ki_r: (bi, ki_r[s], hi)),
                  pl.BlockSpec((1, nh * V_DIM, tk), lambda bi, hi, s, qi_r, ki_r: (bi, hi, ki_r[s]))],
        out_specs=pl.BlockSpec((1, nh * V_DIM, tq), lambda bi, hi, s, qi_r, ki_r: (bi, hi, qi_r[s])),
        scratch_shapes=[pltpu.VMEM((nh, tq), F32), pltpu.VMEM((nh, tq), F32), pltpu.VMEM((nh * V_DIM, tq), F32)],
    )
    return pl.pallas_call(
        functools.partial(_attn_kernel, tq=tq, tk=tk, causal=causal, kv_valid=kv_valid),
        grid_spec=grid_spec,
        out_shape=jax.ShapeDtypeStruct((b, MLA_HEADS * V_DIM, t_q), F32),
        compiler_params=_cparams("parallel", "parallel", "arbitrary"),
        name="attention",
    )(qi, ki, qt, k, vt)


def _attn_out_kernel(ot_ref, h_ref, wo_ref, o_ref):
    o_ref[...] = h_ref[...] + _dot(ot_ref[...].T.astype(BF16), wo_ref[...])


def _attn_out(ot, h, wo, tm):
    t = h.shape[0]
    return pl.pallas_call(
        _attn_out_kernel,
        grid=(t // tm,),
        in_specs=[pl.BlockSpec((MLA_HEADS * V_DIM, tm), lambda i: (0, i)),
                  pl.BlockSpec((tm, D_MODEL), lambda i: (i, 0)), _full((MLA_HEADS * V_DIM, D_MODEL))],
        out_specs=pl.BlockSpec((tm, D_MODEL), lambda i: (i, 0)),
        out_shape=jax.ShapeDtypeStruct((t, D_MODEL), F32),
        compiler_params=_cparams("parallel"),
        name="attn_out",
    )(ot, h, wo)


_STAIR = [(a, b) for a in range(PEER_TOPK) for b in range(PEER_TOPK) if (a + 1) * (b + 1) <= PEER_TOPK]
STAIR_ROWS = 56
NOT_RANKED = PEER_TOPK
PAD_FLAT_INDEX = PEER_TOPK * PEER_TOPK


def _top_values(s):
    vals = []
    x = s
    for _ in range(PEER_TOPK):
        m = jnp.max(x, axis=0, keepdims=True)
        x = jnp.where(x == m, NEG_INF, x)
        vals.append(m)
    return jnp.concatenate(vals, axis=0)


def _count_ge(x, thr):
    return jnp.sum(jnp.where(x >= thr, 1.0, 0.0), axis=0, keepdims=True)


def _route_by_value(s1, s2, sel_a, sel_b, sel_at, fidx):
    v1 = _top_values(s1)
    v2 = []
    kb = jnp.full(s2.shape, float(NOT_RANKED), F32)
    x = s2
    for b in range(PEER_TOPK):
        m = jnp.max(x, axis=0, keepdims=True)
        hit = x == m
        kb = jnp.where(hit, float(b), kb)
        x = jnp.where(hit, NEG_INF, x)
        v2.append(m)
    v2 = jnp.concatenate(v2, axis=0)
    cand = _sel_lhs(sel_a, v1) + _sel_lhs(sel_b, v2)
    cand = jnp.where(fidx < PAD_FLAT_INDEX, cand, NEG_INF)
    tops = _top_values(cand)
    zsum = jnp.sum(jnp.exp(tops - tops[0:1, :]), axis=0, keepdims=True)
    picked = cand >= tops[PEER_TOPK - 1:PEER_TOPK, :]
    cnt = _dot(sel_at, jnp.where(picked, 1.0, 0.0).astype(BF16))
    th = jnp.zeros(s1.shape, F32)
    for a in range(PEER_TOPK):
        th = jnp.where(s1 == v1[a:a + 1, :], cnt[a:a + 1, :], th)
    in1 = s1 >= v1[PEER_TOPK - 1:PEER_TOPK, :]
    in2 = s2 >= v2[PEER_TOPK - 1:PEER_TOPK, :]
    e1 = jnp.where(in1, jnp.exp(s1 - v1[0:1, :]), 0.0) * (1.0 / zsum)
    e2 = jnp.where(in2, jnp.exp(s2 - v2[0:1, :]), 0.0)
    k = float(PEER_TOPK)
    clean = ((jnp.sum(jnp.where(in1, 1.0, 0.0), axis=0, keepdims=True) == k)
             & (jnp.sum(jnp.where(in2, 1.0, 0.0), axis=0, keepdims=True) == k)
             & (jnp.sum(jnp.where(picked, 1.0, 0.0), axis=0, keepdims=True) == k))
    tie = jnp.max(jnp.where(clean, 0.0, 1.0))
    return e1, th, e2, kb, tie


def _topk_columns(s, key_iota):
    rank = jnp.full(s.shape, NOT_RANKED, I32)
    vals = []
    x = s
    for a in range(PEER_TOPK):
        m = jnp.max(x, axis=0, keepdims=True)
        first = jnp.min(jnp.where(x == m, key_iota, PEER_NKEYS), axis=0, keepdims=True)
        hit = key_iota == first
        rank = jnp.where(hit, a, rank)
        x = jnp.where(hit, NEG_INF, x)
        vals.append(m)
    return jnp.concatenate(vals, axis=0), rank


def _route_by_rank(s1, s2, sel_a, sel_b, fidx):
    tm = s1.shape[1]
    key_iota = lax.broadcasted_iota(I32, (PEER_NKEYS, tm), 0)
    a_iota = lax.broadcasted_iota(I32, (PEER_TOPK, tm), 0)
    v1, r1 = _topk_columns(s1, key_iota)
    v2, r2 = _topk_columns(s2, key_iota)
    cand = _sel_lhs(sel_a, v1) + _sel_lhs(sel_b, v2)
    cand = jnp.where(fidx < PAD_FLAT_INDEX, cand, NEG_INF)
    top = None
    zsum = None
    cnt = jnp.zeros((PEER_TOPK, tm), F32)
    for k in range(PEER_TOPK):
        m = jnp.max(cand, axis=0, keepdims=True)
        f = jnp.min(jnp.where(cand == m, fidx, PAD_FLAT_INDEX), axis=0, keepdims=True)
        cand = jnp.where(fidx == f, NEG_INF, cand)
        if k == 0:
            top = m
            zsum = jnp.ones_like(m)
        else:
            zsum = zsum + jnp.exp(m - top)
        cnt = jnp.where(a_iota == (f >> 4), cnt + 1.0, cnt)
    th = jnp.zeros(s1.shape, F32)
    for a in range(PEER_TOPK):
        th = jnp.where(r1 == a, cnt[a:a + 1, :], th)
    e1 = jnp.where(r1 < NOT_RANKED, jnp.exp(s1 - v1[0:1, :]), 0.0) * (1.0 / zsum)
    e2 = jnp.where(r2 < NOT_RANKED, jnp.exp(s2 - v2[0:1, :]), 0.0)
    return e1, th, e2, r2.astype(F32)


def _peer_route_kernel(h_ref, nw_ref, wq_ref, kh_ref, kl_ref, sela_ref, selb_ref, selat_ref, fidx_ref,
                       xn_ref, e1_ref, th_ref, e2_ref, kb_ref, qt_ref):
    xn = _rms(h_ref[...], nw_ref[...]).astype(BF16)
    xn_ref[...] = xn
    qt_ref[...] = _dot_nt(wq_ref[...], xn)

    def one_head(hd, carry):
        scores = []
        for c in range(2):
            hc = 2 * hd + c
            qs = qt_ref[pl.ds(pl.multiple_of(hc * PEER_HALF, PEER_HALF), PEER_HALF), :]
            q_hi = qs.astype(BF16)
            q_lo = (qs - q_hi.astype(F32)).astype(BF16)
            scores.append(_dot(kh_ref[hc], q_hi) + _dot(kh_ref[hc], q_lo) + _dot(kl_ref[hc], q_hi))
        e1, th, e2, kb, tie = _route_by_value(scores[0], scores[1], sela_ref[...], selb_ref[...], selat_ref[...],
                                              fidx_ref[...])
        e1_ref[hd] = e1
        th_ref[hd] = th
        e2_ref[hd] = e2.astype(e2_ref.dtype)
        kb_ref[hd] = kb.astype(kb_ref.dtype)

        @pl.when(tie > 0.0)
        def _():
            e1x, thx, e2x, kbx = _route_by_rank(scores[0], scores[1], sela_ref[...], selb_ref[...], fidx_ref[...])
            e1_ref[hd] = e1x
            th_ref[hd] = thx
            e2_ref[hd] = e2x.astype(e2_ref.dtype)
            kb_ref[hd] = kbx.astype(kb_ref.dtype)

        return carry

    lax.fori_loop(0, PEER_HEADS, one_head, 0)


def _peer_route(h, p, tm):
    t = h.shape[0]
    tab = pl.BlockSpec((PEER_HEADS, PEER_NKEYS, tm), lambda i: (0, 0, i))
    tab_shape = jax.ShapeDtypeStruct((PEER_HEADS, PEER_NKEYS, t), F32)
    fidx = jnp.asarray(np.tile(np.array([a * PEER_TOPK + b for a, b in _STAIR]
                                        + [PAD_FLAT_INDEX] * (STAIR_ROWS - len(_STAIR)), np.int32)[:, None], (1, tm)))
    n_q = 2 * PEER_HEADS * PEER_HALF
    return pl.pallas_call(
        _peer_route_kernel,
        grid=(t // tm,),
        in_specs=[pl.BlockSpec((tm, D_MODEL), lambda i: (i, 0)), _full((1, D_MODEL)),
                  _full((n_q, D_MODEL)),
                  _full((2 * PEER_HEADS, PEER_NKEYS, PEER_HALF)), _full((2 * PEER_HEADS, PEER_NKEYS, PEER_HALF)),
                  _full((STAIR_ROWS, PEER_TOPK)), _full((STAIR_ROWS, PEER_TOPK)), _full((PEER_TOPK, STAIR_ROWS)),
                  _full((STAIR_ROWS, tm))],
        out_specs=[pl.BlockSpec((tm, D_MODEL), lambda i: (i, 0)), tab, tab, tab, tab],
        out_shape=[jax.ShapeDtypeStruct((t, D_MODEL), BF16), tab_shape, tab_shape, tab_shape, tab_shape],
        scratch_shapes=[pltpu.VMEM((n_q, tm), F32)],
        compiler_params=_cparams("parallel"),
        name="peer_route",
    )(h, p["norm_ffn"], p["wq_t"], p["keys_hi"], p["keys_lo"], p["sel_a"], p["sel_b"], p["sel_at"], fidx)


EXPERT_BLOCK = 1024
ROWS_PER_BLOCK = EXPERT_BLOCK // PEER_NKEYS
MIX_GROUP = 256
MIX_PIECE_ROWS = 32


def _gelu_tanh(x):
    return 0.5 * x * (1.0 + jnp.tanh(0.7978845608028654 * (x + 0.044715 * (x * x * x))))


def _peer_mix_kernel(h_ref, xn_ref, e1_ref, th_ref, e2_ref, kb_ref, u_ref, vt_ref, o_ref, wg_ref, acc_ref,
                     ht_ref):
    j = pl.program_id(1)

    @pl.when(j == 0)
    def _():
        acc_ref[...] = jnp.zeros(acc_ref.shape, F32)

    tm = xn_ref.shape[0]
    n_groups = EXPERT_BLOCK // MIX_GROUP
    keys_per_group = MIX_GROUP // PEER_NKEYS

    def expert_inputs(g):
        return _dot_nt(u_ref[g * MIX_GROUP:(g + 1) * MIX_GROUP, :], xn_ref[...])

    ht_next = expert_inputs(0)
    for g in range(n_groups):
        ht_ref[...] = ht_next
        if g + 1 < n_groups:
            ht_next = expert_inputs(g + 1)
        for rr in range(keys_per_group):
            r = g * keys_per_group + rr
            def row_table(ref, hd):
                return jnp.concatenate(
                    [jnp.broadcast_to(ref[hd, r:r + 1, lt * LANES:(lt + 1) * LANES], (MIX_PIECE_ROWS, LANES))
                     for lt in range(tm // LANES)], axis=1)

            for q in range(PEER_NKEYS // MIX_PIECE_ROWS):
                keys = slice(q * MIX_PIECE_ROWS, (q + 1) * MIX_PIECE_ROWS)
                w = None
                for hd in range(PEER_HEADS):
                    hit = kb_ref[hd, keys, :] < row_table(th_ref, hd)
                    term = jnp.where(hit, row_table(e1_ref, hd) * e2_ref[hd, keys, :], 0.0)
                    w = term if w is None else w + term
                lo = rr * PEER_NKEYS + q * MIX_PIECE_ROWS
                act = _gelu_tanh(ht_ref[lo:lo + MIX_PIECE_ROWS, :])
                lo = r * PEER_NKEYS + q * MIX_PIECE_ROWS
                wg_ref[lo:lo + MIX_PIECE_ROWS, :] = (w * act).astype(BF16)
    acc_ref[...] += _dot(vt_ref[...], wg_ref[...])

    @pl.when(j == pl.num_programs(1) - 1)
    def _():
        o_ref[...] = h_ref[...] + acc_ref[...].T


def _peer_mix(h, xn, e1, th, e2, kb, u, vt, tm):
    t = h.shape[0]
    n_exp = u.shape[0]
    rowblk = pl.BlockSpec((PEER_HEADS, ROWS_PER_BLOCK, tm), lambda i, j: (0, j, i))
    fullblk = pl.BlockSpec((PEER_HEADS, PEER_NKEYS, tm), lambda i, j: (0, 0, i))
    return pl.pallas_call(
        _peer_mix_kernel,
        grid=(t // tm, n_exp // EXPERT_BLOCK),
        in_specs=[pl.BlockSpec((tm, D_MODEL), lambda i, j: (i, 0)), pl.BlockSpec((tm, D_MODEL), lambda i, j: (i, 0)),
                  rowblk, rowblk, fullblk, fullblk,
                  pl.BlockSpec((EXPERT_BLOCK, D_MODEL), lambda i, j: (j, 0)),
                  pl.BlockSpec((D_MODEL, EXPERT_BLOCK), lambda i, j: (0, j))],
        out_specs=pl.BlockSpec((tm, D_MODEL), lambda i, j: (i, 0)),
        out_shape=jax.ShapeDtypeStruct((t, D_MODEL), F32),
        scratch_shapes=[pltpu.VMEM((EXPERT_BLOCK, tm), BF16), pltpu.VMEM((D_MODEL, tm), F32),
                        pltpu.VMEM((MIX_GROUP, tm), F32)],
        compiler_params=_cparams("parallel", "arbitrary"),
        name="peer_mix",
    )(h, xn, e1, th, e2, kb, u, vt)


def _ple_kernel(h_ref, p_ref, nw_ref, wg_ref, wp_ref, fnw_ref, o_ref, *, final):
    h = h_ref[...]
    gate = 1.0 / (1.0 + jnp.exp(-_dot(_rms(h, nw_ref[...]).astype(BF16), wg_ref[...])))
    out = h + gate * _dot(p_ref[...].astype(BF16), wp_ref[...])
    if final:
        out = _rms(out, fnw_ref[...])
    o_ref[...] = out


def _ple(h, pemb, p, fnw, final, tm):
    t = h.shape[0]
    pd = pemb.shape[1]
    return pl.pallas_call(
        functools.partial(_ple_kernel, final=final),
        grid=(t // tm,),
        in_specs=[pl.BlockSpec((tm, D_MODEL), lambda i: (i, 0)), pl.BlockSpec((tm, pd), lambda i: (i, 0)),
                  _full((1, D_MODEL)), _full((D_MODEL, D_MODEL)), _full((pd, D_MODEL)), _full((1, D_MODEL))],
        out_specs=pl.BlockSpec((tm, D_MODEL), lambda i: (i, 0)),
        out_shape=jax.ShapeDtypeStruct((t, D_MODEL), F32),
        compiler_params=_cparams("parallel"),
        name="ple",
    )(h, pemb, p["ple_norm"], p["w_ple_gate"], p["w_ple_proj"], fnw)


def _row(v, width=None):
    v = v.reshape(1, -1).astype(F32)
    if width is not None and v.shape[1] < width:
        v = jnp.pad(v, ((0, 0), (0, width - v.shape[1])))
    return v


def _prep_even(w, e):
    win = w["w_in_e"][e]
    c0, c1, c2 = SSD_WIDTH, SSD_WIDTH + CONV_CH, SSD_WIDTH + CONV_CH + SSD_HEADS
    w_cat = jnp.concatenate([win[:, :c0], win[:, c0:c1], win[:, c2:], win[:, c1:c2],
                             jnp.zeros((D_MODEL, LANES - SSD_HEADS), F32)], axis=1).astype(BF16)
    expand = np.zeros((LANES, SSD_WIDTH), np.float32)
    for hh in range(SSD_HEADS):
        expand[hh, hh * SSD_HEAD_DIM:(hh + 1) * SSD_HEAD_DIM] = 1.0
    return dict(
        w_in=w_cat, w_dt_t=win[:, c1:c2].T.astype(BF16),
        conv_w=w["conv_w"][e], conv_b=_row(w["conv_b"][e]),
        dt_bias=_row(w["dt_bias"][e], LANES),
        dt_bias_t=jnp.broadcast_to(w["dt_bias"][e][:, None], (SSD_HEADS, SSD_BLOCK)),
        a_log=_row(w["a_log"][e], LANES),
        a_log_t=jnp.broadcast_to(w["a_log"][e][:, None], (SSD_HEADS, SSD_BLOCK)),
        d_skip=_row(jnp.repeat(w["d_skip"][e], SSD_HEAD_DIM)), ssd_norm_w=_row(w["ssd_norm_w"][e]),
        expand=jnp.asarray(expand, BF16),
        pool_w=w["pool_w"][e].astype(BF16), pool_scale=_row(w["pool_scale"][e]),
        w_out=w["w_out_e"][e].astype(BF16))


def _prep_odd(w, o):
    half = QK_ROPE // 2
    win = w["w_in_o"][o]
    w_kpe = win[:, Q_LORA + KV_LORA:]
    w_kpe_sw = jnp.concatenate([-w_kpe[:, half:], w_kpe[:, :half]], axis=1)
    zpad = jnp.zeros((D_MODEL, LANES - QK_ROPE), F32)
    w_cat = jnp.concatenate([win[:, :Q_LORA + KV_LORA], w_kpe, zpad, w_kpe_sw, zpad], axis=1).astype(BF16)
    wuq = w["w_uq"][o].reshape(Q_LORA, MLA_HEADS, QK_NOPE + QK_ROPE)
    nope, pe = wuq[..., :QK_NOPE], wuq[..., QK_NOPE:]
    pe_sw = jnp.concatenate([-pe[..., half:], pe[..., :half]], axis=-1)
    z32 = jnp.zeros((Q_LORA, MLA_HEADS, HEAD_SLOT - QK_NOPE - QK_ROPE), F32)
    w_qa = jnp.concatenate([nope, pe, z32], axis=-1).reshape(Q_LORA, QK_ROWS).astype(BF16)
    w_qb = jnp.concatenate([jnp.zeros_like(nope), pe_sw, z32], axis=-1).reshape(Q_LORA, QK_ROWS).astype(BF16)
    wukv = w["w_ukv"][o].reshape(KV_LORA, MLA_HEADS, QK_NOPE + V_DIM)
    w_k = jnp.concatenate([wukv[..., :QK_NOPE], jnp.zeros((KV_LORA, MLA_HEADS, HEAD_SLOT - QK_NOPE), F32)],
                          axis=-1).reshape(KV_LORA, QK_ROWS).astype(BF16)
    w_vt = wukv[..., QK_NOPE:].reshape(KV_LORA, MLA_HEADS * V_DIM).T.astype(BF16)
    r128 = np.zeros((half, LANES), np.float32)
    rq = np.zeros((half, QK_ROWS), np.float32)
    rq_mask = np.zeros((1, QK_ROWS), np.float32)
    r_k = np.zeros((QK_ROPE, QK_ROWS), np.float32)
    for i in range(half):
        r128[i, i] = r128[i, half + i] = 1.0
    for hh in range(MLA_HEADS):
        base = hh * HEAD_SLOT
        rq_mask[0, base:base + QK_NOPE] = 1.0
        for i in range(half):
            rq[i, base + QK_NOPE + i] = rq[i, base + QK_NOPE + half + i] = 1.0
        for i in range(QK_ROPE):
            r_k[i, base + QK_NOPE + i] = 1.0
    return dict(
        w_in=w_cat, q_norm=_row(w["q_norm"][o]), kv_norm=_row(w["kv_norm"][o]), w_qa=w_qa, w_qb=w_qb,
        r128=jnp.asarray(r128, BF16), rq=jnp.asarray(rq, BF16), rq_mask=jnp.asarray(rq_mask),
        w_k=w_k, r_k=jnp.asarray(r_k, BF16), w_vt=w_vt, w_out=w["w_out_o"][o].astype(BF16))


def _hi_lo(x):
    hi = x.astype(BF16)
    return hi, (x - hi.astype(F32)).astype(BF16)


def _prep_common(w, i):
    keys_hi, keys_lo = _hi_lo(w["peer_keys"][i].reshape(2 * PEER_HEADS, PEER_NKEYS, PEER_HALF))
    sel_a = np.zeros((STAIR_ROWS, PEER_TOPK), np.float32)
    sel_b = np.zeros((STAIR_ROWS, PEER_TOPK), np.float32)
    for row_i, (a, b) in enumerate(_STAIR):
        sel_a[row_i, a] = 1.0
        sel_b[row_i, b] = 1.0
    return dict(
        norm_mix=_row(w["norm_mix"][i]), norm_ffn=_row(w["norm_ffn"][i]), ple_norm=_row(w["ple_norm"][i]),
        wq_t=w["peer_wq"][i].T.astype(BF16), keys_hi=keys_hi, keys_lo=keys_lo,
        sel_a=jnp.asarray(sel_a, BF16), sel_b=jnp.asarray(sel_b, BF16), sel_at=jnp.asarray(sel_a.T, BF16),
        peer_u=w["peer_u"][i].astype(BF16), peer_vt=w["peer_v"][i].T.astype(BF16),
        w_ple_gate=w["w_ple_gate"][i].astype(BF16), w_ple_proj=w["w_ple_proj"][i].astype(BF16))


def _pad_axis(x, axis, size):
    if x.shape[axis] == size:
        return x
    pads = [(0, 0)] * x.ndim
    pads[axis] = (0, size - x.shape[axis])
    return jnp.pad(x, pads)


def _round_up(n, m):
    return -(-n // m) * m


def _even_layer(h, conv_st, ssm_st, pool_st, pos0, p, cfg):
    b, l, _ = h.shape
    z, xbc, u, dt, dtt = _even_in(h.reshape(b * l, D_MODEL), p["norm_mix"], p["w_in"], p["w_dt_t"], cfg["tm"])
    z, xbc, u, dt = (a.reshape(b, l, -1) for a in (z, xbc, u, dt))
    dtt = dtt.reshape(SSD_HEADS, b, l).transpose(1, 0, 2)
    lp = _round_up(l, SSD_BLOCK)
    r = SSD_HEADS // SSD_GROUPS
    s0 = ssm_st.astype(F32).reshape(b, SSD_GROUPS, r, SSD_HEAD_DIM, SSD_STATE).transpose(0, 1, 4, 2, 3)
    s0 = s0.reshape(b, SSD_GROUPS, SSD_STATE, r * SSD_HEAD_DIM)
    hist = jnp.pad(conv_st.astype(F32), ((0, 0), (8 - (CONV_WIDTH - 1), 0), (0, 0)))
    y, s_fin = _ssd(_pad_axis(xbc, 1, lp), _pad_axis(z, 1, lp), _pad_axis(dt, 1, lp), _pad_axis(dtt, 2, lp),
                    hist, s0, p, l)
    new_ssm = s_fin.reshape(b, SSD_GROUPS, SSD_STATE, r, SSD_HEAD_DIM).transpose(0, 1, 3, 4, 2)
    new_ssm = new_ssm.reshape(b, SSD_HEADS, SSD_HEAD_DIM, SSD_STATE)
    new_conv = jnp.concatenate([conv_st.astype(F32), xbc], axis=1)[:, -(CONV_WIDTH - 1):]
    new_pool = jnp.concatenate([pool_st.astype(F32), u], axis=1)[:, -POOL_HIST:]
    phist = jnp.pad(pool_st.astype(F32), ((0, 0), (16 - POOL_HIST, 0), (0, 0)))
    h_new = _pool_out(u, phist, y[:, :l], h, p, pos0, cfg["tm_seq"])
    return h_new, new_conv, new_ssm, new_pool


def _odd_layer(h, ckv_hist, kpe_hist, pos0, p, cfg):
    b, l, _ = h.shape
    half = QK_ROPE // 2
    pos = (pos0 + jnp.arange(l)).astype(F32)
    inv = ROPE_THETA ** (-jnp.arange(half, dtype=F32) / half)
    ang = pos[:, None] * inv[None, :]
    cos = jnp.tile(jnp.cos(ang), (b, 1))
    sin = jnp.tile(jnp.sin(ang), (b, 1))
    hf = h.reshape(b * l, D_MODEL)
    ckv, kpe, qt = _odd_in(hf, cos, sin, p, cfg["tm"])
    ckv = ckv.reshape(b, l, KV_LORA)
    kpe = kpe.reshape(b, l, QK_ROPE)
    ckv_all = jnp.concatenate([ckv_hist.astype(F32), ckv], axis=1)
    kpe_all = jnp.concatenate([kpe_hist.astype(F32), kpe], axis=1)
    n_keys = ckv_all.shape[1]
    causal = ckv_hist.shape[1] == 0
    if causal:
        tq, tk, nk_pad, lq = cfg["tq"], cfg["tk"], n_keys, l
    else:
        nk_pad = _round_up(n_keys, ATTN_SUB)
        tq, tk, lq = LANES, nk_pad, _round_up(l, LANES)
    k, vt = _kv_expand(_pad_axis(ckv_all, 1, nk_pad), _pad_axis(kpe_all, 1, nk_pad), p, cfg["tm_kv"])
    qt = _pad_axis(qt.reshape(QK_ROWS, b, l).transpose(1, 0, 2), 2, lq)
    ot = _attention(qt, k, vt, tq, tk, causal, None if causal else n_keys)
    ot = ot[:, :, :l].transpose(1, 0, 2).reshape(MLA_HEADS * V_DIM, b * l)
    h_new = _attn_out(ot, hf, p["w_out"], cfg["tm"]).reshape(b, l, D_MODEL)
    return h_new, ckv, kpe


def _prep_layers(w):
    layers = []
    for i in range(DEPTH):
        pc = _prep_common(w, i)
        layers.append(dict(pc, **(_prep_even(w, i // 2) if i % 2 == 0 else _prep_odd(w, i // 2))))
    return layers


def _trunk(x, pemb, conv_st, ssm_st, pool_st, ckv_h, kpe_h, pos0, layers, fnw, cfg):
    b, l, _ = x.shape
    h = x.astype(F32)
    convs, ssms, pools, ckvs, kpes = [], [], [], [], []
    for i in range(DEPTH):
        pc = layers[i]
        if i % 2 == 0:
            e = i // 2
            h, c_new, s_new, p_new = _even_layer(h, conv_st[e], ssm_st[e], pool_st[e], pos0, pc, cfg)
            convs.append(c_new)
            ssms.append(s_new)
            pools.append(p_new)
        else:
            o = i // 2
            h, ckv_new, kpe_new = _odd_layer(h, ckv_h[o], kpe_h[o], pos0, pc, cfg)
            ckvs.append(ckv_new)
            kpes.append(kpe_new)
        hf = h.reshape(b * l, D_MODEL)
        xn, e1, th, e2, kb = _peer_route(hf, pc, cfg["tm_route"])
        hf = _peer_mix(hf, xn, e1, th, e2, kb, pc["peer_u"], pc["peer_vt"], cfg["tm_mix"])
        hf = _ple(hf, pemb[i].reshape(b * l, -1), pc, fnw, i == DEPTH - 1, cfg["tm"])
        h = hf.reshape(b, l, D_MODEL)
    return h, jnp.stack(convs), jnp.stack(ssms), jnp.stack(pools), jnp.stack(ckvs), jnp.stack(kpes)


def _tile(n, pref):
    t = min(pref, n)
    while n % t:
        t -= 8
    return t


def _config(b, l):
    t = b * l
    return dict(tm=_tile(t, 512), tm_seq=_tile(l, 512), tm_route=_tile(t, 512), tm_mix=_tile(t, 1024),
                tq=_tile(l, 512), tk=_tile(l, 512), tm_kv=LANES)


def kernel(x_prompt, x_sample, state_conv, state_ssm, state_pool, cache_ckv, cache_kpe, p_prompt, p_sample,
           norm_mix, norm_ffn, ple_norm, final_norm, w_in_e, conv_w, conv_b, dt_bias, a_log, d_skip, ssd_norm_w,
           pool_w, pool_scale, w_out_e, w_in_o, q_norm, kv_norm, w_uq, w_ukv, w_out_o, peer_wq, peer_keys, peer_u,
           peer_v, w_ple_proj, w_ple_gate):
    w = dict(norm_mix=norm_mix, norm_ffn=norm_ffn, ple_norm=ple_norm, final_norm=final_norm, w_in_e=w_in_e,
             conv_w=conv_w, conv_b=conv_b, dt_bias=dt_bias, a_log=a_log, d_skip=d_skip, ssd_norm_w=ssd_norm_w,
             pool_w=pool_w, pool_scale=pool_scale, w_out_e=w_out_e, w_in_o=w_in_o, q_norm=q_norm, kv_norm=kv_norm,
             w_uq=w_uq, w_ukv=w_ukv, w_out_o=w_out_o, peer_wq=peer_wq, peer_keys=peer_keys, peer_u=peer_u,
             peer_v=peer_v, w_ple_proj=w_ple_proj, w_ple_gate=w_ple_gate)
    n_even, n_odd = (DEPTH + 1) // 2, DEPTH // 2
    b0, l0, _ = x_prompt.shape
    conv0 = jnp.zeros((n_even, b0, CONV_WIDTH - 1, CONV_CH), F32)
    ssm0 = jnp.zeros((n_even, b0, SSD_HEADS, SSD_HEAD_DIM, SSD_STATE), F32)
    pool0 = jnp.zeros((n_even, b0, POOL_HIST, SSD_WIDTH), F32)
    ckv0 = jnp.zeros((n_odd, b0, 0, KV_LORA), F32)
    kpe0 = jnp.zeros((n_odd, b0, 0, QK_ROPE), F32)
    layers = _prep_layers(w)
    fnw = _row(final_norm)
    outs_p = _trunk(x_prompt, p_prompt, conv0, ssm0, pool0, ckv0, kpe0, 0, layers, fnw, _config(b0, l0))
    b1, l1, _ = x_sample.shape
    pos0 = cache_ckv.shape[2]
    outs_s = _trunk(x_sample, p_sample, state_conv, state_ssm, state_pool, cache_ckv, cache_kpe, pos0, layers, fnw,
                    _config(b1, l1))
    return (outs_p[0], outs_s[0]) + outs_p[1:] + outs_s[1:]
```

```python
import functools

import numpy as np
import jax
import jax.numpy as jnp
from jax import lax
from jax.experimental import pallas as pl
from jax.experimental.pallas import tpu as pltpu

F32 = jnp.float32
BF16 = jnp.bfloat16
I32 = jnp.int32

D_MODEL = 1024
DEPTH = 4
CHUNK = 64
EPS = 1e-6
SSD_HEAD_DIM = 64
SSD_HEADS = 16
SSD_WIDTH = 1024
SSD_GROUPS = 2
SSD_STATE = 128
CONV_WIDTH = 4
CONV_CH = 1536
POOL_WINDOWS = (2, 4, 8, 16)
POOL_GROUP_DIM = 256
POOL_HIST = 15
QK_NOPE = 64
QK_ROPE = 32
V_DIM = 64
MLA_HEADS = 16
Q_LORA = 256
KV_LORA = 256
ROPE_THETA = 10000.0
PEER_HEADS = 8
PEER_NKEYS = 128
PEER_HALF = 128
PEER_TOPK = 16

LANES = 128
BF16_ROWS = 16
VMEM_LIMIT_BYTES = 56 * 1024 * 1024
SSD_BLOCK = 128
HEAD_SLOT = 128

NEG_INF = float("-inf")


def _cparams(*sem):
    return pltpu.CompilerParams(dimension_semantics=sem, vmem_limit_bytes=VMEM_LIMIT_BYTES)


def _dot(a, b):
    return jnp.dot(a, b, preferred_element_type=F32)


def _dot_nt(a, b):
    return lax.dot_general(a, b, (((1,), (1,)), ((), ())), preferred_element_type=F32)


def _split3(x):
    hi = x.astype(BF16)
    r = x - hi.astype(F32)
    mid = r.astype(BF16)
    lo = (r - mid.astype(F32)).astype(BF16)
    return hi, mid, lo


def _sel_rhs(x, sel):
    hi, mid, lo = _split3(x)
    return _dot(hi, sel) + _dot(mid, sel) + _dot(lo, sel)


def _sel_lhs(sel, x):
    hi, mid, lo = _split3(x)
    return _dot(sel, hi) + _dot(sel, mid) + _dot(sel, lo)


def _rms(x, w):
    return x * lax.rsqrt(jnp.mean(x * x, axis=-1, keepdims=True) + EPS) * w


def _silu(x):
    return x * (1.0 / (1.0 + jnp.exp(-x)))


def _softplus(x):
    return jnp.maximum(x, 0.0) + jnp.log(1.0 + jnp.exp(-jnp.abs(x)))


def _full(shape):
    nd = len(shape)
    return pl.BlockSpec(shape, lambda *_: (0,) * nd)


EVEN_COLS = SSD_WIDTH + CONV_CH + SSD_WIDTH + LANES


def _even_in_kernel(h_ref, nw_ref, w_ref, wdt_ref, z_ref, xbc_ref, u_ref, dt_ref, dtt_ref):
    xn = _rms(h_ref[...], nw_ref[...]).astype(BF16)
    proj = _dot(xn, w_ref[...])
    z_ref[...] = proj[:, :SSD_WIDTH]
    xbc_ref[...] = proj[:, SSD_WIDTH:SSD_WIDTH + CONV_CH]
    u_ref[...] = proj[:, SSD_WIDTH + CONV_CH:2 * SSD_WIDTH + CONV_CH]
    dt_ref[...] = proj[:, 2 * SSD_WIDTH + CONV_CH:]
    dtt_ref[...] = _dot_nt(wdt_ref[...], xn)


def _even_in(h, nw, w, wdt, tm):
    t = h.shape[0]
    row = lambda n: pl.BlockSpec((tm, n), lambda i: (i, 0))
    return pl.pallas_call(
        _even_in_kernel,
        grid=(t // tm,),
        in_specs=[row(D_MODEL), _full((1, D_MODEL)), _full((D_MODEL, EVEN_COLS)), _full((SSD_HEADS, D_MODEL))],
        out_specs=[row(SSD_WIDTH), row(CONV_CH), row(SSD_WIDTH), row(LANES),
                   pl.BlockSpec((SSD_HEADS, tm), lambda i: (0, i))],
        out_shape=[jax.ShapeDtypeStruct((t, SSD_WIDTH), F32), jax.ShapeDtypeStruct((t, CONV_CH), F32),
                   jax.ShapeDtypeStruct((t, SSD_WIDTH), F32), jax.ShapeDtypeStruct((t, LANES), F32),
                   jax.ShapeDtypeStruct((SSD_HEADS, t), F32)],
        compiler_params=_cparams("parallel"),
        name="even_in",
    )(h, nw, w, wdt)


def _ssd_kernel(xbc_ref, z_ref, dt_ref, dtt_ref, hist_ref, s0_ref, cw_ref, cb_ref, bias_ref, biast_ref,
                a_ref, at_ref, dskip_ref, nw_ref, expand_ref, y_ref, sfin_ref, ext_ref, st_ref, *, valid_len):
    q = SSD_BLOCK
    c = pl.program_id(1)

    @pl.when(c == 0)
    def _():
        ext_ref[0:8, :] = hist_ref[0]
        st_ref[...] = s0_ref[0]

    ext_ref[8:8 + q, :] = xbc_ref[0]
    conv = cb_ref[...] + cw_ref[0:1, :] * ext_ref[5:5 + q, :]
    for j in range(1, CONV_WIDTH):
        conv = conv + cw_ref[j:j + 1, :] * ext_ref[5 + j:5 + j + q, :]
    ext_ref[0:8, :] = ext_ref[q:q + 8, :]
    xbc = _silu(conv)
    xs = xbc[:, :SSD_WIDTH]
    xs_b = xs.astype(BF16)

    row_i = lax.broadcasted_iota(I32, (q, q), 0)
    col_i = lax.broadcasted_iota(I32, (q, q), 1)
    tril = (row_i >= col_i).astype(BF16)
    triu = (row_i <= col_i).astype(BF16)
    ones = jnp.ones((q, q), BF16)
    tok_ok = (c * q + lax.broadcasted_iota(I32, (q, LANES), 0)) < valid_len
    dt_tok = jnp.where(tok_ok, _softplus(dt_ref[0] + bias_ref[...]), 0.0)
    cs_tok = _sel_lhs(tril, dt_tok * -jnp.exp(a_ref[...]))
    lane_ok = (c * q + lax.broadcasted_iota(I32, (SSD_HEADS, q), 1)) < valid_len
    dt_t = jnp.where(lane_ok, _softplus(dtt_ref[0] + biast_ref[...]), 0.0)
    da_t = dt_t * -jnp.exp(at_ref[...])
    cs_t = _sel_rhs(da_t, triu)
    tot_t = _sel_rhs(da_t, ones)
    w_t = dt_t * jnp.exp(tot_t - cs_t)
    cs_exp = _sel_rhs(cs_tok, expand_ref[...])
    ecs = jnp.exp(cs_exp)
    dchunk = ecs[q - 1:q, :]

    lane_lo = lax.broadcasted_iota(I32, (q, LANES), 1) < SSD_HEAD_DIM
    causal = row_i >= col_i
    r = SSD_HEADS // SSD_GROUPS
    gw = r * SSD_HEAD_DIM
    y_parts = []
    for g in range(SSD_GROUPS):
        bm = xbc[:, SSD_WIDTH + g * SSD_STATE:SSD_WIDTH + (g + 1) * SSD_STATE]
        cm = xbc[:, SSD_WIDTH + (SSD_GROUPS + g) * SSD_STATE:SSD_WIDTH + (SSD_GROUPS + g + 1) * SSD_STATE]
        cm_b = cm.astype(BF16)
        cb = _dot_nt(cm_b, bm.astype(BF16))
        bm_t = bm.T
        st_g = st_ref[g]
        y_off = _dot(cm_b, st_g.astype(BF16)) * ecs[:, g * gw:(g + 1) * gw]
        yd, up = [], []
        for pr in range(r // 2):
            res_y, res_u = [], []
            xs_pair = xs_b[:, g * gw + pr * LANES:g * gw + (pr + 1) * LANES]
            for k in range(2):
                hh = g * r + 2 * pr + k
                seg = cs_tok[:, hh:hh + 1] - cs_t[hh:hh + 1, :]
                lmat = jnp.where(causal, jnp.exp(seg), 0.0)
                m = (cb * lmat * dt_t[hh:hh + 1, :]).astype(BF16)
                res_y.append(_dot(m, xs_pair))
                res_u.append(_dot((bm_t * w_t[hh:hh + 1, :]).astype(BF16), xs_pair))
            yd.append(jnp.where(lane_lo, res_y[0], res_y[1]))
            up.append(jnp.where(lane_lo, res_u[0], res_u[1]))
        y_parts.append(jnp.concatenate(yd, axis=1) + y_off)
        st_ref[g] = dchunk[:, g * gw:(g + 1) * gw] * st_g + jnp.concatenate(up, axis=1)

    y = jnp.concatenate(y_parts, axis=1) + dskip_ref[...] * xs
    y = y * _silu(z_ref[0])
    outs = []
    for g in range(SSD_GROUPS):
        yg = y[:, g * gw:(g + 1) * gw]
        outs.append(yg * lax.rsqrt(jnp.mean(yg * yg, axis=-1, keepdims=True) + EPS))
    y_ref[0] = (jnp.concatenate(outs, axis=1) * nw_ref[...]).astype(y_ref.dtype)

    @pl.when(c == pl.num_programs(1) - 1)
    def _():
        sfin_ref[0] = st_ref[...]


def _ssd(xbc, z, dt, dtt, hist, s0, p, valid_len):
    b, l, _ = xbc.shape
    q = SSD_BLOCK
    n_state = SSD_GROUPS * SSD_STATE
    gw = SSD_WIDTH // SSD_GROUPS
    blk = lambda n: pl.BlockSpec((1, q, n), lambda i, c: (i, c, 0))
    per_b = lambda s: pl.BlockSpec((1,) + s, lambda i, c: (i,) + (0,) * len(s))
    return pl.pallas_call(
        functools.partial(_ssd_kernel, valid_len=valid_len),
        grid=(b, l // q),
        in_specs=[blk(CONV_CH), blk(SSD_WIDTH), blk(LANES),
                  pl.BlockSpec((1, SSD_HEADS, q), lambda i, c: (i, 0, c)),
                  per_b((8, CONV_CH)), per_b((SSD_GROUPS, SSD_STATE, gw)),
                  _full((CONV_WIDTH, CONV_CH)), _full((1, CONV_CH)), _full((1, LANES)), _full((SSD_HEADS, q)),
                  _full((1, LANES)), _full((SSD_HEADS, q)), _full((1, SSD_WIDTH)), _full((1, SSD_WIDTH)),
                  _full((LANES, SSD_WIDTH))],
        out_specs=[blk(SSD_WIDTH), per_b((SSD_GROUPS, SSD_STATE, gw))],
        out_shape=[jax.ShapeDtypeStruct((b, l, SSD_WIDTH), BF16),
                   jax.ShapeDtypeStruct((b, SSD_GROUPS, SSD_STATE, gw), F32)],
        scratch_shapes=[pltpu.VMEM((q + 8, CONV_CH), F32), pltpu.VMEM((SSD_GROUPS, SSD_STATE, gw), F32)],
        compiler_params=_cparams("parallel", "arbitrary"),
        name="ssd_scan",
    )(xbc, z, dt, dtt, hist, s0, p["conv_w"], p["conv_b"], p["dt_bias"], p["dt_bias_t"], p["a_log"], p["a_log_t"],
      p["d_skip"], p["ssd_norm_w"], p["expand"])


def _pool_out_kernel(u_ref, hist_ref, y_ref, h_ref, pw_ref, ps_ref, wo_ref, o_ref, ext_ref, *, pos0, tm):
    i = pl.program_id(1)

    @pl.when(i == 0)
    def _():
        ext_ref[0:16, :] = hist_ref[0]

    ext_ref[16:16 + tm, :] = u_ref[0]
    pos = pos0 + i * tm + lax.broadcasted_iota(I32, (tm, POOL_GROUP_DIM), 0)
    acc = h_ref[0] + _dot(y_ref[0], wo_ref[0:SSD_WIDTH, :])
    yps = []
    for gi, wsz in enumerate(POOL_WINDOWS):
        c0 = gi * POOL_GROUP_DIM
        cur = ext_ref[16:16 + tm, c0:c0 + POOL_GROUP_DIM]
        tot = cur
        for j in range(1, wsz):
            tot = tot + ext_ref[16 - j:16 - j + tm, c0:c0 + POOL_GROUP_DIM]
        cnt = jnp.minimum(pos + 1, wsz).astype(F32)
        pooled = tot / cnt - cur
        yps.append(_dot(pooled.astype(BF16), pw_ref[gi]))
    yp = (jnp.concatenate(yps, axis=1) * ps_ref[...]).astype(BF16)
    o_ref[0] = acc + _dot(yp, wo_ref[SSD_WIDTH:, :])
    ext_ref[0:16, :] = ext_ref[tm:tm + 16, :]


def _pool_out(u, hist, y, h, p, pos0, tm):
    b, l, _ = u.shape
    blk = lambda n: pl.BlockSpec((1, tm, n), lambda bi, i: (bi, i, 0))
    return pl.pallas_call(
        functools.partial(_pool_out_kernel, pos0=pos0, tm=tm),
        grid=(b, l // tm),
        in_specs=[blk(SSD_WIDTH), pl.BlockSpec((1, 16, SSD_WIDTH), lambda bi, i: (bi, 0, 0)), blk(SSD_WIDTH),
                  blk(D_MODEL), _full((len(POOL_WINDOWS), POOL_GROUP_DIM, POOL_GROUP_DIM)), _full((1, SSD_WIDTH)),
                  _full((2 * SSD_WIDTH, D_MODEL))],
        out_specs=blk(D_MODEL),
        out_shape=jax.ShapeDtypeStruct((b, l, D_MODEL), F32),
        scratch_shapes=[pltpu.VMEM((tm + 16, SSD_WIDTH), F32)],
        compiler_params=_cparams("parallel", "arbitrary"),
        name="pool_out",
    )(u, hist, y, h, p["pool_w"], p["pool_scale"], p["w_out"])


ODD_COLS = Q_LORA + KV_LORA + 2 * LANES
QK_ROWS = MLA_HEADS * HEAD_SLOT


def _odd_in_kernel(h_ref, nw_ref, w_ref, qn_ref, kvn_ref, wa_ref, wb_ref, cos_ref, sin_ref, r128_ref, rq_ref,
                   rqm_ref, ckv_ref, kpe_ref, qt_ref, *, scale):
    xn = _rms(h_ref[...], nw_ref[...]).astype(BF16)
    proj = _dot(xn, w_ref[...])
    cq = _rms(proj[:, :Q_LORA], qn_ref[...]).astype(BF16)
    ckv_ref[...] = _rms(proj[:, Q_LORA:Q_LORA + KV_LORA], kvn_ref[...])
    cos = cos_ref[...]
    sin = sin_ref[...]
    kpe = proj[:, Q_LORA + KV_LORA:Q_LORA + KV_LORA + LANES]
    kpe_sw = proj[:, Q_LORA + KV_LORA + LANES:]
    kpe_rot = kpe * _sel_rhs(cos, r128_ref[...]) + kpe_sw * _sel_rhs(sin, r128_ref[...])
    kpe_ref[...] = kpe_rot[:, :QK_ROPE]
    cos_q = (_sel_rhs(cos, rq_ref[...]) + rqm_ref[...]) * scale
    sin_q = _sel_rhs(sin, rq_ref[...]) * scale
    q = _dot(cq, wa_ref[...]) * cos_q + _dot(cq, wb_ref[...]) * sin_q
    qt_ref[...] = q.T.astype(qt_ref.dtype)


def _odd_in(h, cos, sin, p, tm):
    t = h.shape[0]
    row = lambda n: pl.BlockSpec((tm, n), lambda i: (i, 0))
    half = QK_ROPE // 2
    scale = float((QK_NOPE + QK_ROPE) ** -0.5 * np.log2(np.e))
    return pl.pallas_call(
        functools.partial(_odd_in_kernel, scale=scale),
        grid=(t // tm,),
        in_specs=[row(D_MODEL), _full((1, D_MODEL)), _full((D_MODEL, ODD_COLS)), _full((1, Q_LORA)),
                  _full((1, KV_LORA)), _full((Q_LORA, QK_ROWS)), _full((Q_LORA, QK_ROWS)), row(half), row(half),
                  _full((half, LANES)), _full((half, QK_ROWS)), _full((1, QK_ROWS))],
        out_specs=[row(KV_LORA), row(QK_ROPE), pl.BlockSpec((QK_ROWS, tm), lambda i: (0, i))],
        out_shape=[jax.ShapeDtypeStruct((t, KV_LORA), F32), jax.ShapeDtypeStruct((t, QK_ROPE), F32),
                   jax.ShapeDtypeStruct((QK_ROWS, t), BF16)],
        compiler_params=_cparams("parallel"),
        name="odd_in",
    )(h, p["norm_mix"], p["w_in"], p["q_norm"], p["kv_norm"], p["w_qa"], p["w_qb"], cos, sin, p["r128"], p["rq"],
      p["rq_mask"])


def _kv_expand_kernel(ckv_ref, kpe_ref, wk_ref, rk_ref, wvt_ref, k_ref, vt_ref):
    ckv = ckv_ref[0].astype(BF16)
    k_ref[0] = (_dot(ckv, wk_ref[...]) + _dot(kpe_ref[0].astype(BF16), rk_ref[...])).astype(k_ref.dtype)
    vt_ref[0] = _dot_nt(wvt_ref[...], ckv).astype(vt_ref.dtype)


def _kv_expand(ckv, kpe, p, tm):
    b, t, _ = ckv.shape
    vrows = MLA_HEADS * V_DIM
    return pl.pallas_call(
        _kv_expand_kernel,
        grid=(b, t // tm),
        in_specs=[pl.BlockSpec((1, tm, KV_LORA), lambda bi, i: (bi, i, 0)),
                  pl.BlockSpec((1, tm, QK_ROPE), lambda bi, i: (bi, i, 0)),
                  _full((KV_LORA, QK_ROWS)), _full((QK_ROPE, QK_ROWS)), _full((vrows, KV_LORA))],
        out_specs=[pl.BlockSpec((1, tm, QK_ROWS), lambda bi, i: (bi, i, 0)),
                   pl.BlockSpec((1, vrows, tm), lambda bi, i: (bi, 0, i))],
        out_shape=[jax.ShapeDtypeStruct((b, t, QK_ROWS), BF16), jax.ShapeDtypeStruct((b, vrows, t), BF16)],
        compiler_params=_cparams("parallel", "parallel"),
        name="kv_expand",
    )(ckv, kpe, p["w_k"], p["r_k"], p["w_vt"])


ATTN_HEADS_PER_STEP = 4
ATTN_SUB = 256
ATTN_LOOKAHEAD = 5


def _attn_kernel(qi_ref, ki_ref, qt_ref, k_ref, vt_ref, o_ref, m_ref, l_ref, acc_ref, *, tq, tk, causal,
                 kv_valid):
    s_idx = pl.program_id(2)
    qi = qi_ref[s_idx]
    ki = ki_ref[s_idx]
    nh = ATTN_HEADS_PER_STEP

    @pl.when(ki == 0)
    def _():
        m_ref[...] = jnp.full(m_ref.shape, NEG_INF, F32)
        l_ref[...] = jnp.zeros(l_ref.shape, F32)
        acc_ref[...] = jnp.zeros(acc_ref.shape, F32)

    sk, sq = min(ATTN_SUB, tk), min(ATTN_SUB, tq)

    def step(diagonal):
        items = []
        for ks in range(tk // sk):
            for hh in range(nh):
                for qs in range(tq // sq):
                    if causal and diagonal and ks * sk >= (qs + 1) * sq:
                        continue
                    if not causal and kv_valid is not None and ks * sk >= kv_valid:
                        continue
                    items.append((ks, hh, qs))

        def needs_mask(ks, qs):
            if causal:
                return diagonal and (ks + 1) * sk > qs * sq
            return kv_valid is not None and (ks + 1) * sk > kv_valid

        def scores(ks, hh, qs):
            return _dot(k_ref[0, ks * sk:(ks + 1) * sk, hh * HEAD_SLOT:(hh + 1) * HEAD_SLOT],
                        qt_ref[0, hh * HEAD_SLOT:(hh + 1) * HEAD_SLOT, qs * sq:(qs + 1) * sq])

        state = {}
        for hh in range(nh):
            for qs in range(tq // sq):
                cols = slice(qs * sq, (qs + 1) * sq)
                state[hh, qs] = (m_ref[hh:hh + 1, cols], l_ref[hh:hh + 1, cols],
                                 acc_ref[hh * V_DIM:(hh + 1) * V_DIM, cols])
        pending = [scores(*it) for it in items[:ATTN_LOOKAHEAD]]
        for idx, (ks, hh, qs) in enumerate(items):
            st = pending.pop(0)
            if idx + ATTN_LOOKAHEAD < len(items):
                pending.append(scores(*items[idx + ATTN_LOOKAHEAD]))
            if needs_mask(ks, qs):
                kpos = ks * sk + lax.broadcasted_iota(I32, (sk, sq), 0)
                if causal:
                    qpos = qs * sq + lax.broadcasted_iota(I32, (sk, sq), 1)
                    ok = (kpos // CHUNK) <= (qpos // CHUNK)
                else:
                    ok = kpos < kv_valid
                st = jnp.where(ok, st, NEG_INF)
            m_prev, l_prev, acc_prev = state[hh, qs]
            m_new = jnp.maximum(m_prev, jnp.max(st, axis=0, keepdims=True))
            alpha = jnp.exp2(m_prev - m_new)
            pt = jnp.exp2(st - m_new)
            l_new = alpha * l_prev + jnp.sum(pt, axis=0, keepdims=True)
            pv = _dot(vt_ref[0, hh * V_DIM:(hh + 1) * V_DIM, ks * sk:(ks + 1) * sk], pt.astype(BF16))
            state[hh, qs] = (m_new, l_new, alpha * acc_prev + pv)
        for (hh, qs), (m_new, l_new, acc_new) in state.items():
            cols = slice(qs * sq, (qs + 1) * sq)
            m_ref[hh:hh + 1, cols] = m_new
            l_ref[hh:hh + 1, cols] = l_new
            acc_ref[hh * V_DIM:(hh + 1) * V_DIM, cols] = acc_new

    def finish():
        for hh in range(nh):
            vrows = slice(hh * V_DIM, (hh + 1) * V_DIM)
            o_ref[0, vrows, :] = (acc_ref[vrows, :] / l_ref[hh:hh + 1, :]).astype(o_ref.dtype)

    if causal:
        last = ((qi + 1) * tq - 1) // tk

        @pl.when(ki < last)
        def _():
            step(False)

        @pl.when(ki == last)
        def _():
            step(True)
            finish()
    else:
        step(False)
        finish()


def _attention(qt, k, vt, tq, tk, causal, kv_valid):
    b, _, t_q = qt.shape
    t_k = k.shape[1]
    nq, nk = t_q // tq, t_k // tk
    nh = ATTN_HEADS_PER_STEP
    if causal:
        assert tk == tq
        pairs = [(a, c) for a in range(nq) for c in range(((a + 1) * tq - 1) // tk + 1)]
    else:
        assert nk == 1
        pairs = [(a, 0) for a in range(nq)]
    qi = jnp.asarray(np.array([a for a, _ in pairs], np.int32))
    ki = jnp.asarray(np.array([c for _, c in pairs], np.int32))
    grid_spec = pltpu.PrefetchScalarGridSpec(
        num_scalar_prefetch=2,
        grid=(b, MLA_HEADS // nh, len(pairs)),
        in_specs=[pl.BlockSpec((1, nh * HEAD_SLOT, tq), lambda bi, hi, s, qi_r, ki_r: (bi, hi, qi_r[s])),
                  pl.BlockSpec((1, tk, nh * HEAD_SLOT), lambda bi, hi, s, qi_r, ki_r: (bi, ki_r[s], hi)),
                  pl.BlockSpec((1, nh * V_DIM, tk), lambda bi, hi, s, qi_r, ki_r: (bi, hi, ki_r[s]))],
        out_specs=pl.BlockSpec((1, nh * V_DIM, tq), lambda bi, hi, s, qi_r, ki_r: (bi, hi, qi_r[s])),
        scratch_shapes=[pltpu.VMEM((nh, tq), F32), pltpu.VMEM((nh, tq), F32), pltpu.VMEM((nh * V_DIM, tq), F32)],
    )
    return pl.pallas_call(
        functools.partial(_attn_kernel, tq=tq, tk=tk, causal=causal, kv_valid=kv_valid),
        grid_spec=grid_spec,
        out_shape=jax.ShapeDtypeStruct((b, MLA_HEADS * V_DIM, t_q), F32),
        compiler_params=_cparams("parallel", "parallel", "arbitrary"),
        name="attention",
    )(qi, ki, qt, k, vt)


def _attn_out_kernel(ot_ref, h_ref, wo_ref, o_ref):
    o_ref[...] = h_ref[...] + _dot(ot_ref[...].T.astype(BF16), wo_ref[...])


def _attn_out(ot, h, wo, tm):
    t = h.shape[0]
    return pl.pallas_call(
        _attn_out_kernel,
        grid=(t // tm,),
        in_specs=[pl.BlockSpec((MLA_HEADS * V_DIM, tm), lambda i: (0, i)),
                  pl.BlockSpec((tm, D_MODEL), lambda i: (i, 0)), _full((MLA_HEADS * V_DIM, D_MODEL))],
        out_specs=pl.BlockSpec((tm, D_MODEL), lambda i: (i, 0)),
        out_shape=jax.ShapeDtypeStruct((t, D_MODEL), F32),
        compiler_params=_cparams("parallel"),
        name="attn_out",
    )(ot, h, wo)


_STAIR = [(a, b) for a in range(PEER_TOPK) for b in range(PEER_TOPK) if (a + 1) * (b + 1) <= PEER_TOPK]
STAIR_ROWS = 56
NOT_RANKED = PEER_TOPK
PAD_FLAT_INDEX = PEER_TOPK * PEER_TOPK


def _top_values(s):
    vals = []
    x = s
    for _ in range(PEER_TOPK):
        m = jnp.max(x, axis=0, keepdims=True)
        x = jnp.where(x == m, NEG_INF, x)
        vals.append(m)
    return jnp.concatenate(vals, axis=0)


def _count_ge(x, thr):
    return jnp.sum(jnp.where(x >= thr, 1.0, 0.0), axis=0, keepdims=True)


def _route_by_value(s1, s2, sel_a, sel_b, sel_at, fidx):
    v1 = _top_values(s1)
    v2 = []
    kb = jnp.full(s2.shape, float(NOT_RANKED), F32)
    x = s2
    for b in range(PEER_TOPK):
        m = jnp.max(x, axis=0, keepdims=True)
        hit = x == m
        kb = jnp.where(hit, float(b), kb)
        x = jnp.where(hit, NEG_INF, x)
        v2.append(m)
    v2 = jnp.concatenate(v2, axis=0)
    cand = _sel_lhs(sel_a, v1) + _sel_lhs(sel_b, v2)
    cand = jnp.where(fidx < PAD_FLAT_INDEX, cand, NEG_INF)
    tops = _top_values(cand)
    zsum = jnp.sum(jnp.exp(tops - tops[0:1, :]), axis=0, keepdims=True)
    picked = cand >= tops[PEER_TOPK - 1:PEER_TOPK, :]
    cnt = _dot(sel_at, jnp.where(picked, 1.0, 0.0).astype(BF16))
    th = jnp.zeros(s1.shape, F32)
    for a in range(PEER_TOPK):
        th = jnp.where(s1 == v1[a:a + 1, :], cnt[a:a + 1, :], th)
    in1 = s1 >= v1[PEER_TOPK - 1:PEER_TOPK, :]
    in2 = s2 >= v2[PEER_TOPK - 1:PEER_TOPK, :]
    e1 = jnp.where(in1, jnp.exp(s1 - v1[0:1, :]), 0.0) * (1.0 / zsum)
    e2 = jnp.where(in2, jnp.exp(s2 - v2[0:1, :]), 0.0)
    k = float(PEER_TOPK)
    clean = ((jnp.sum(jnp.where(in1, 1.0, 0.0), axis=0, keepdims=True) == k)
             & (jnp.sum(jnp.where(in2, 1.0, 0.0), axis=0, keepdims=True) == k)
             & (jnp.sum(jnp.where(picked, 1.0, 0.0), axis=0, keepdims=True) == k))
    tie = jnp.max(jnp.where(clean, 0.0, 1.0))
    return e1, th, e2, kb, tie


def _topk_columns(s, key_iota):
    rank = jnp.full(s.shape, NOT_RANKED, I32)
    vals = []
    x = s
    for a in range(PEER_TOPK):
        m = jnp.max(x, axis=0, keepdims=True)
        first = jnp.min(jnp.where(x == m, key_iota, PEER_NKEYS), axis=0, keepdims=True)
        hit = key_iota == first
        rank = jnp.where(hit, a, rank)
        x = jnp.where(hit, NEG_INF, x)
        vals.append(m)
    return jnp.concatenate(vals, axis=0), rank


def _route_by_rank(s1, s2, sel_a, sel_b, fidx):
    tm = s1.shape[1]
    key_iota = lax.broadcasted_iota(I32, (PEER_NKEYS, tm), 0)
    a_iota = lax.broadcasted_iota(I32, (PEER_TOPK, tm), 0)
    v1, r1 = _topk_columns(s1, key_iota)
    v2, r2 = _topk_columns(s2, key_iota)
    cand = _sel_lhs(sel_a, v1) + _sel_lhs(sel_b, v2)
    cand = jnp.where(fidx < PAD_FLAT_INDEX, cand, NEG_INF)
    top = None
    zsum = None
    cnt = jnp.zeros((PEER_TOPK, tm), F32)
    for k in range(PEER_TOPK):
        m = jnp.max(cand, axis=0, keepdims=True)
        f = jnp.min(jnp.where(cand == m, fidx, PAD_FLAT_INDEX), axis=0, keepdims=True)
        cand = jnp.where(fidx == f, NEG_INF, cand)
        if k == 0:
            top = m
            zsum = jnp.ones_like(m)
        else:
            zsum = zsum + jnp.exp(m - top)
        cnt = jnp.where(a_iota == (f >> 4), cnt + 1.0, cnt)
    th = jnp.zeros(s1.shape, F32)
    for a in range(PEER_TOPK):
        th = jnp.where(r1 == a, cnt[a:a + 1, :], th)
    e1 = jnp.where(r1 < NOT_RANKED, jnp.exp(s1 - v1[0:1, :]), 0.0) * (1.0 / zsum)
    e2 = jnp.where(r2 < NOT_RANKED, jnp.exp(s2 - v2[0:1, :]), 0.0)
    return e1, th, e2, r2.astype(F32)


def _peer_route_kernel(h_ref, nw_ref, wq_ref, kh_ref, kl_ref, sela_ref, selb_ref, selat_ref, fidx_ref,
                       xn_ref, e1_ref, th_ref, e2_ref, kb_ref, qt_ref):
    xn = _rms(h_ref[...], nw_ref[...]).astype(BF16)
    xn_ref[...] = xn
    qt_ref[...] = _dot_nt(wq_ref[...], xn)

    def one_head(hd, carry):
        scores = []
        for c in range(2):
            hc = 2 * hd + c
            qs = qt_ref[pl.ds(pl.multiple_of(hc * PEER_HALF, PEER_HALF), PEER_HALF), :]
            q_hi = qs.astype(BF16)
            q_lo = (qs - q_hi.astype(F32)).astype(BF16)
            scores.append(_dot(kh_ref[hc], q_hi) + _dot(kh_ref[hc], q_lo) + _dot(kl_ref[hc], q_hi))
        e1, th, e2, kb, tie = _route_by_value(scores[0], scores[1], sela_ref[...], selb_ref[...], selat_ref[...],
                                              fidx_ref[...])
        e1_ref[hd] = e1
        th_ref[hd] = th
        e2_ref[hd] = e2.astype(e2_ref.dtype)
        kb_ref[hd] = kb.astype(kb_ref.dtype)

        @pl.when(tie > 0.0)
        def _():
            e1x, thx, e2x, kbx = _route_by_rank(scores[0], scores[1], sela_ref[...], selb_ref[...], fidx_ref[...])
            e1_ref[hd] = e1x
            th_ref[hd] = thx
            e2_ref[hd] = e2x.astype(e2_ref.dtype)
            kb_ref[hd] = kbx.astype(kb_ref.dtype)

        return carry

    lax.fori_loop(0, PEER_HEADS, one_head, 0)


def _peer_route(h, p, tm):
    t = h.shape[0]
    tab = pl.BlockSpec((PEER_HEADS, PEER_NKEYS, tm), lambda i: (0, 0, i))
    tab_shape = jax.ShapeDtypeStruct((PEER_HEADS, PEER_NKEYS, t), F32)
    tab_shape_b = jax.ShapeDtypeStruct((PEER_HEADS, PEER_NKEYS, t), BF16)
    fidx = jnp.asarray(np.tile(np.array([a * PEER_TOPK + b for a, b in _STAIR]
                                        + [PAD_FLAT_INDEX] * (STAIR_ROWS - len(_STAIR)), np.int32)[:, None], (1, tm)))
    n_q = 2 * PEER_HEADS * PEER_HALF
    return pl.pallas_call(
        _peer_route_kernel,
        grid=(t // tm,),
        in_specs=[pl.BlockSpec((tm, D_MODEL), lambda i: (i, 0)), _full((1, D_MODEL)),
                  _full((n_q, D_MODEL)),
                  _full((2 * PEER_HEADS, PEER_NKEYS, PEER_HALF)), _full((2 * PEER_HEADS, PEER_NKEYS, PEER_HALF)),
                  _full((STAIR_ROWS, PEER_TOPK)), _full((STAIR_ROWS, PEER_TOPK)), _full((PEER_TOPK, STAIR_ROWS)),
                  _full((STAIR_ROWS, tm))],
        out_specs=[pl.BlockSpec((tm, D_MODEL), lambda i: (i, 0)), tab, tab, tab, tab],
        out_shape=[jax.ShapeDtypeStruct((t, D_MODEL), BF16), tab_shape, tab_shape, tab_shape_b, tab_shape_b],
        scratch_shapes=[pltpu.VMEM((n_q, tm), F32)],
        compiler_params=_cparams("parallel"),
        name="peer_route",
    )(h, p["norm_ffn"], p["wq_t"], p["keys_hi"], p["keys_lo"], p["sel_a"], p["sel_b"], p["sel_at"], fidx)


EXPERT_BLOCK = 1024
ROWS_PER_BLOCK = EXPERT_BLOCK // PEER_NKEYS
MIX_GROUP = 256
MIX_PIECE_SLABS = 2


def _gelu_tanh(x):
    return 0.5 * x * (1.0 + jnp.tanh(0.7978845608028654 * (x + 0.044715 * (x * x * x))))


def _peer_mix_kernel(h_ref, xn_ref, e1_rows_ref, th_rows_ref, e2_ref, kb_ref, u_ref, vt_ref, o_ref, wg_ref, acc_ref,
                     ht_ref, e1_ref, th_ref):
    j = pl.program_id(1)

    @pl.when(j == 0)
    def _():
        acc_ref[...] = jnp.zeros(acc_ref.shape, F32)

    tm = xn_ref.shape[0]
    n_groups = EXPERT_BLOCK // MIX_GROUP
    keys_per_group = MIX_GROUP // PEER_NKEYS
    piece_rows = MIX_PIECE_SLABS * BF16_ROWS
    for hd in range(PEER_HEADS):
        for r in range(ROWS_PER_BLOCK):
            th_ref[hd, r] = jnp.broadcast_to(th_rows_ref[hd, r:r + 1, :], (BF16_ROWS, tm)).astype(BF16)
            e1_ref[hd, r] = jnp.broadcast_to(e1_rows_ref[hd, r:r + 1, :], (BF16_ROWS, tm)).astype(BF16)

    def expert_inputs(g):
        return _dot_nt(u_ref[g * MIX_GROUP:(g + 1) * MIX_GROUP, :], xn_ref[...])

    ht_ref[0] = expert_inputs(0)
    for g in range(n_groups):
        ht_cur = ht_ref.at[g % 2]
        if g + 1 < n_groups:
            ht_ref[(g + 1) % 2] = expert_inputs(g + 1)
        for rr in range(keys_per_group):
            r = g * keys_per_group + rr
            for q in range(PEER_NKEYS // piece_rows):
                slabs = slice(q * MIX_PIECE_SLABS, (q + 1) * MIX_PIECE_SLABS)
                w = None
                for hd in range(PEER_HEADS):
                    hit = kb_ref[hd, slabs] < th_ref[hd, r][None]
                    term = jnp.where(hit, e1_ref[hd, r][None] * e2_ref[hd, slabs], jnp.zeros((), BF16))
                    w = term if w is None else w + term
                lo = rr * PEER_NKEYS + q * piece_rows
                act = _gelu_tanh(ht_cur[lo:lo + piece_rows, :]).astype(BF16)
                lo = r * PEER_NKEYS + q * piece_rows
                wg_ref[lo:lo + piece_rows, :] = w.reshape(piece_rows, tm) * act
    acc_ref[...] += _dot(vt_ref[...], wg_ref[...])

    @pl.when(j == pl.num_programs(1) - 1)
    def _():
        o_ref[...] = h_ref[...] + acc_ref[...].T


def _peer_mix(h, xn, e1, th, e2, kb, u, vt, tm):
    t = h.shape[0]
    n_exp = u.shape[0]
    rowblk = pl.BlockSpec((PEER_HEADS, ROWS_PER_BLOCK, tm), lambda i, j: (0, j, i))
    n_slabs = PEER_NKEYS // BF16_ROWS
    fullblk = pl.BlockSpec((PEER_HEADS, n_slabs, BF16_ROWS, tm), lambda i, j: (0, 0, 0, i))
    e2 = e2.reshape(PEER_HEADS, n_slabs, BF16_ROWS, t)
    kb = kb.reshape(PEER_HEADS, n_slabs, BF16_ROWS, t)
    row_scratch = pltpu.VMEM((PEER_HEADS, ROWS_PER_BLOCK, BF16_ROWS, tm), BF16)
    return pl.pallas_call(
        _peer_mix_kernel,
        grid=(t // tm, n_exp // EXPERT_BLOCK),
        in_specs=[pl.BlockSpec((tm, D_MODEL), lambda i, j: (i, 0)), pl.BlockSpec((tm, D_MODEL), lambda i, j: (i, 0)),
                  rowblk, rowblk, fullblk, fullblk,
                  pl.BlockSpec((EXPERT_BLOCK, D_MODEL), lambda i, j: (j, 0)),
                  pl.BlockSpec((D_MODEL, EXPERT_BLOCK), lambda i, j: (0, j))],
        out_specs=pl.BlockSpec((tm, D_MODEL), lambda i, j: (i, 0)),
        out_shape=jax.ShapeDtypeStruct((t, D_MODEL), F32),
        scratch_shapes=[pltpu.VMEM((EXPERT_BLOCK, tm), BF16), pltpu.VMEM((D_MODEL, tm), F32),
                        pltpu.VMEM((2, MIX_GROUP, tm), F32), row_scratch, row_scratch],
        compiler_params=_cparams("parallel", "arbitrary"),
        name="peer_mix",
    )(h, xn, e1, th, e2, kb, u, vt)


def _ple_kernel(h_ref, p_ref, nw_ref, wg_ref, wp_ref, fnw_ref, o_ref, *, final):
    h = h_ref[...]
    gate = 1.0 / (1.0 + jnp.exp(-_dot(_rms(h, nw_ref[...]).astype(BF16), wg_ref[...])))
    out = h + gate * _dot(p_ref[...].astype(BF16), wp_ref[...])
    if final:
        out = _rms(out, fnw_ref[...])
    o_ref[...] = out


def _ple(h, pemb, p, fnw, final, tm):
    t = h.shape[0]
    pd = pemb.shape[1]
    return pl.pallas_call(
        functools.partial(_ple_kernel, final=final),
        grid=(t // tm,),
        in_specs=[pl.BlockSpec((tm, D_MODEL), lambda i: (i, 0)), pl.BlockSpec((tm, pd), lambda i: (i, 0)),
                  _full((1, D_MODEL)), _full((D_MODEL, D_MODEL)), _full((pd, D_MODEL)), _full((1, D_MODEL))],
        out_specs=pl.BlockSpec((tm, D_MODEL), lambda i: (i, 0)),
        out_shape=jax.ShapeDtypeStruct((t, D_MODEL), F32),
        compiler_params=_cparams("parallel"),
        name="ple",
    )(h, pemb, p["ple_norm"], p["w_ple_gate"], p["w_ple_proj"], fnw)


def _row(v, width=None):
    v = v.reshape(1, -1).astype(F32)
    if width is not None and v.shape[1] < width:
        v = jnp.pad(v, ((0, 0), (0, width - v.shape[1])))
    return v


def _prep_even(w, e):
    win = w["w_in_e"][e]
    c0, c1, c2 = SSD_WIDTH, SSD_WIDTH + CONV_CH, SSD_WIDTH + CONV_CH + SSD_HEADS
    w_cat = jnp.concatenate([win[:, :c0], win[:, c0:c1], win[:, c2:], win[:, c1:c2],
                             jnp.zeros((D_MODEL, LANES - SSD_HEADS), F32)], axis=1).astype(BF16)
    expand = np.zeros((LANES, SSD_WIDTH), np.float32)
    for hh in range(SSD_HEADS):
        expand[hh, hh * SSD_HEAD_DIM:(hh + 1) * SSD_HEAD_DIM] = 1.0
    return dict(
        w_in=w_cat, w_dt_t=win[:, c1:c2].T.astype(BF16),
        conv_w=w["conv_w"][e], conv_b=_row(w["conv_b"][e]),
        dt_bias=_row(w["dt_bias"][e], LANES),
        dt_bias_t=jnp.broadcast_to(w["dt_bias"][e][:, None], (SSD_HEADS, SSD_BLOCK)),
        a_log=_row(w["a_log"][e], LANES),
        a_log_t=jnp.broadcast_to(w["a_log"][e][:, None], (SSD_HEADS, SSD_BLOCK)),
        d_skip=_row(jnp.repeat(w["d_skip"][e], SSD_HEAD_DIM)), ssd_norm_w=_row(w["ssd_norm_w"][e]),
        expand=jnp.asarray(expand, BF16),
        pool_w=w["pool_w"][e].astype(BF16), pool_scale=_row(w["pool_scale"][e]),
        w_out=w["w_out_e"][e].astype(BF16))


def _prep_odd(w, o):
    half = QK_ROPE // 2
    win = w["w_in_o"][o]
    w_kpe = win[:, Q_LORA + KV_LORA:]
    w_kpe_sw = jnp.concatenate([-w_kpe[:, half:], w_kpe[:, :half]], axis=1)
    zpad = jnp.zeros((D_MODEL, LANES - QK_ROPE), F32)
    w_cat = jnp.concatenate([win[:, :Q_LORA + KV_LORA], w_kpe, zpad, w_kpe_sw, zpad], axis=1).astype(BF16)
    wuq = w["w_uq"][o].reshape(Q_LORA, MLA_HEADS, QK_NOPE + QK_ROPE)
    nope, pe = wuq[..., :QK_NOPE], wuq[..., QK_NOPE:]
    pe_sw = jnp.concatenate([-pe[..., half:], pe[..., :half]], axis=-1)
    z32 = jnp.zeros((Q_LORA, MLA_HEADS, HEAD_SLOT - QK_NOPE - QK_ROPE), F32)
    w_qa = jnp.concatenate([nope, pe, z32], axis=-1).reshape(Q_LORA, QK_ROWS).astype(BF16)
    w_qb = jnp.concatenate([jnp.zeros_like(nope), pe_sw, z32], axis=-1).reshape(Q_LORA, QK_ROWS).astype(BF16)
    wukv = w["w_ukv"][o].reshape(KV_LORA, MLA_HEADS, QK_NOPE + V_DIM)
    w_k = jnp.concatenate([wukv[..., :QK_NOPE], jnp.zeros((KV_LORA, MLA_HEADS, HEAD_SLOT - QK_NOPE), F32)],
                          axis=-1).reshape(KV_LORA, QK_ROWS).astype(BF16)
    w_vt = wukv[..., QK_NOPE:].reshape(KV_LORA, MLA_HEADS * V_DIM).T.astype(BF16)
    r128 = np.zeros((half, LANES), np.float32)
    rq = np.zeros((half, QK_ROWS), np.float32)
    rq_mask = np.zeros((1, QK_ROWS), np.float32)
    r_k = np.zeros((QK_ROPE, QK_ROWS), np.float32)
    for i in range(half):
        r128[i, i] = r128[i, half + i] = 1.0
    for hh in range(MLA_HEADS):
        base = hh * HEAD_SLOT
        rq_mask[0, base:base + QK_NOPE] = 1.0
        for i in range(half):
            rq[i, base + QK_NOPE + i] = rq[i, base + QK_NOPE + half + i] = 1.0
        for i in range(QK_ROPE):
            r_k[i, base + QK_NOPE + i] = 1.0
    return dict(
        w_in=w_cat, q_norm=_row(w["q_norm"][o]), kv_norm=_row(w["kv_norm"][o]), w_qa=w_qa, w_qb=w_qb,
        r128=jnp.asarray(r128, BF16), rq=jnp.asarray(rq, BF16), rq_mask=jnp.asarray(rq_mask),
        w_k=w_k, r_k=jnp.asarray(r_k, BF16), w_vt=w_vt, w_out=w["w_out_o"][o].astype(BF16))


def _hi_lo(x):
    hi = x.astype(BF16)
    return hi, (x - hi.astype(F32)).astype(BF16)


def _prep_common(w, i):
    keys_hi, keys_lo = _hi_lo(w["peer_keys"][i].reshape(2 * PEER_HEADS, PEER_NKEYS, PEER_HALF))
    sel_a = np.zeros((STAIR_ROWS, PEER_TOPK), np.float32)
    sel_b = np.zeros((STAIR_ROWS, PEER_TOPK), np.float32)
    for row_i, (a, b) in enumerate(_STAIR):
        sel_a[row_i, a] = 1.0
        sel_b[row_i, b] = 1.0
    return dict(
        norm_mix=_row(w["norm_mix"][i]), norm_ffn=_row(w["norm_ffn"][i]), ple_norm=_row(w["ple_norm"][i]),
        wq_t=w["peer_wq"][i].T.astype(BF16), keys_hi=keys_hi, keys_lo=keys_lo,
        sel_a=jnp.asarray(sel_a, BF16), sel_b=jnp.asarray(sel_b, BF16), sel_at=jnp.asarray(sel_a.T, BF16),
        peer_u=w["peer_u"][i].astype(BF16), peer_vt=w["peer_v"][i].T.astype(BF16),
        w_ple_gate=w["w_ple_gate"][i].astype(BF16), w_ple_proj=w["w_ple_proj"][i].astype(BF16))


def _pad_axis(x, axis, size):
    if x.shape[axis] == size:
        return x
    pads = [(0, 0)] * x.ndim
    pads[axis] = (0, size - x.shape[axis])
    return jnp.pad(x, pads)


def _round_up(n, m):
    return -(-n // m) * m


def _even_layer(h, conv_st, ssm_st, pool_st, pos0, p, cfg):
    b, l, _ = h.shape
    z, xbc, u, dt, dtt = _even_in(h.reshape(b * l, D_MODEL), p["norm_mix"], p["w_in"], p["w_dt_t"], cfg["tm"])
    z, xbc, u, dt = (a.reshape(b, l, -1) for a in (z, xbc, u, dt))
    dtt = dtt.reshape(SSD_HEADS, b, l).transpose(1, 0, 2)
    lp = _round_up(l, SSD_BLOCK)
    r = SSD_HEADS // SSD_GROUPS
    s0 = ssm_st.astype(F32).reshape(b, SSD_GROUPS, r, SSD_HEAD_DIM, SSD_STATE).transpose(0, 1, 4, 2, 3)
    s0 = s0.reshape(b, SSD_GROUPS, SSD_STATE, r * SSD_HEAD_DIM)
    hist = jnp.pad(conv_st.astype(F32), ((0, 0), (8 - (CONV_WIDTH - 1), 0), (0, 0)))
    y, s_fin = _ssd(_pad_axis(xbc, 1, lp), _pad_axis(z, 1, lp), _pad_axis(dt, 1, lp), _pad_axis(dtt, 2, lp),
                    hist, s0, p, l)
    new_ssm = s_fin.reshape(b, SSD_GROUPS, SSD_STATE, r, SSD_HEAD_DIM).transpose(0, 1, 3, 4, 2)
    new_ssm = new_ssm.reshape(b, SSD_HEADS, SSD_HEAD_DIM, SSD_STATE)
    new_conv = jnp.concatenate([conv_st.astype(F32), xbc], axis=1)[:, -(CONV_WIDTH - 1):]
    new_pool = jnp.concatenate([pool_st.astype(F32), u], axis=1)[:, -POOL_HIST:]
    phist = jnp.pad(pool_st.astype(F32), ((0, 0), (16 - POOL_HIST, 0), (0, 0)))
    h_new = _pool_out(u, phist, y[:, :l], h, p, pos0, cfg["tm_seq"])
    return h_new, new_conv, new_ssm, new_pool


def _odd_layer(h, ckv_hist, kpe_hist, pos0, p, cfg):
    b, l, _ = h.shape
    half = QK_ROPE // 2
    pos = (pos0 + jnp.arange(l)).astype(F32)
    inv = ROPE_THETA ** (-jnp.arange(half, dtype=F32) / half)
    ang = pos[:, None] * inv[None, :]
    cos = jnp.tile(jnp.cos(ang), (b, 1))
    sin = jnp.tile(jnp.sin(ang), (b, 1))
    hf = h.reshape(b * l, D_MODEL)
    ckv, kpe, qt = _odd_in(hf, cos, sin, p, cfg["tm"])
    ckv = ckv.reshape(b, l, KV_LORA)
    kpe = kpe.reshape(b, l, QK_ROPE)
    ckv_all = jnp.concatenate([ckv_hist.astype(F32), ckv], axis=1)
    kpe_all = jnp.concatenate([kpe_hist.astype(F32), kpe], axis=1)
    n_keys = ckv_all.shape[1]
    causal = ckv_hist.shape[1] == 0
    if causal:
        tq, tk, nk_pad, lq = cfg["tq"], cfg["tk"], n_keys, l
    else:
        nk_pad = _round_up(n_keys, ATTN_SUB)
        tq, tk, lq = LANES, nk_pad, _round_up(l, LANES)
    k, vt = _kv_expand(_pad_axis(ckv_all, 1, nk_pad), _pad_axis(kpe_all, 1, nk_pad), p, cfg["tm_kv"])
    qt = _pad_axis(qt.reshape(QK_ROWS, b, l).transpose(1, 0, 2), 2, lq)
    ot = _attention(qt, k, vt, tq, tk, causal, None if causal else n_keys)
    ot = ot[:, :, :l].transpose(1, 0, 2).reshape(MLA_HEADS * V_DIM, b * l)
    h_new = _attn_out(ot, hf, p["w_out"], cfg["tm"]).reshape(b, l, D_MODEL)
    return h_new, ckv, kpe


def _prep_layers(w):
    layers = []
    for i in range(DEPTH):
        pc = _prep_common(w, i)
        layers.append(dict(pc, **(_prep_even(w, i // 2) if i % 2 == 0 else _prep_odd(w, i // 2))))
    return layers


def _trunk(x, pemb, conv_st, ssm_st, pool_st, ckv_h, kpe_h, pos0, layers, fnw, cfg):
    b, l, _ = x.shape
    h = x.astype(F32)
    convs, ssms, pools, ckvs, kpes = [], [], [], [], []
    for i in range(DEPTH):
        pc = layers[i]
        if i % 2 == 0:
            e = i // 2
            h, c_new, s_new, p_new = _even_layer(h, conv_st[e], ssm_st[e], pool_st[e], pos0, pc, cfg)
            convs.append(c_new)
            ssms.append(s_new)
            pools.append(p_new)
        else:
            o = i // 2
            h, ckv_new, kpe_new = _odd_layer(h, ckv_h[o], kpe_h[o], pos0, pc, cfg)
            ckvs.append(ckv_new)
            kpes.append(kpe_new)
        hf = h.reshape(b * l, D_MODEL)
        xn, e1, th, e2, kb = _peer_route(hf, pc, cfg["tm_route"])
        hf = _peer_mix(hf, xn, e1, th, e2, kb, pc["peer_u"], pc["peer_vt"], cfg["tm_mix"])
        hf = _ple(hf, pemb[i].reshape(b * l, -1), pc, fnw, i == DEPTH - 1, cfg["tm"])
        h = hf.reshape(b, l, D_MODEL)
    return h, jnp.stack(convs), jnp.stack(ssms), jnp.stack(pools), jnp.stack(ckvs), jnp.stack(kpes)


def _tile(n, pref):
    t = min(pref, n)
    while n % t:
        t -= 8
    return t


def _config(b, l):
    t = b * l
    return dict(tm=_tile(t, 512), tm_seq=_tile(l, 512), tm_route=_tile(t, 512), tm_mix=_tile(t, 1024),
                tq=_tile(l, 1024), tk=_tile(l, 1024), tm_kv=LANES)


def kernel(x_prompt, x_sample, state_conv, state_ssm, state_pool, cache_ckv, cache_kpe, p_prompt, p_sample,
           norm_mix, norm_ffn, ple_norm, final_norm, w_in_e, conv_w, conv_b, dt_bias, a_log, d_skip, ssd_norm_w,
           pool_w, pool_scale, w_out_e, w_in_o, q_norm, kv_norm, w_uq, w_ukv, w_out_o, peer_wq, peer_keys, peer_u,
           peer_v, w_ple_proj, w_ple_gate):
    w = dict(norm_mix=norm_mix, norm_ffn=norm_ffn, ple_norm=ple_norm, final_norm=final_norm, w_in_e=w_in_e,
             conv_w=conv_w, conv_b=conv_b, dt_bias=dt_bias, a_log=a_log, d_skip=d_skip, ssd_norm_w=ssd_norm_w,
             pool_w=pool_w, pool_scale=pool_scale, w_out_e=w_out_e, w_in_o=w_in_o, q_norm=q_norm, kv_norm=kv_norm,
             w_uq=w_uq, w_ukv=w_ukv, w_out_o=w_out_o, peer_wq=peer_wq, peer_keys=peer_keys, peer_u=peer_u,
             peer_v=peer_v, w_ple_proj=w_ple_proj, w_ple_gate=w_ple_gate)
    n_even, n_odd = (DEPTH + 1) // 2, DEPTH // 2
    b0, l0, _ = x_prompt.shape
    conv0 = jnp.zeros((n_even, b0, CONV_WIDTH - 1, CONV_CH), F32)
    ssm0 = jnp.zeros((n_even, b0, SSD_HEADS, SSD_HEAD_DIM, SSD_STATE), F32)
    pool0 = jnp.zeros((n_even, b0, POOL_HIST, SSD_WIDTH), F32)
    ckv0 = jnp.zeros((n_odd, b0, 0, KV_LORA), F32)
    kpe0 = jnp.zeros((n_odd, b0, 0, QK_ROPE), F32)
    layers = _prep_layers(w)
    fnw = _row(final_norm)
    outs_p = _trunk(x_prompt, p_prompt, conv0, ssm0, pool0, ckv0, kpe0, 0, layers, fnw, _config(b0, l0))
    b1, l1, _ = x_sample.shape
    pos0 = cache_ckv.shape[2]
    outs_s = _trunk(x_sample, p_sample, state_conv, state_ssm, state_pool, cache_ckv, cache_kpe, pos0, layers, fnw,
                    _config(b1, l1))
    return (outs_p[0], outs_s[0]) + outs_p[1:] + outs_s[1:]
```

```python
import functools

import numpy as np
import jax
import jax.numpy as jnp
from jax import lax
from jax.experimental import pallas as pl
from jax.experimental.pallas import tpu as pltpu

F32 = jnp.float32
BF16 = jnp.bfloat16
I32 = jnp.int32

D_MODEL = 1024
DEPTH = 4
CHUNK = 64
EPS = 1e-6
SSD_HEAD_DIM = 64
SSD_HEADS = 16
SSD_WIDTH = 1024
SSD_GROUPS = 2
SSD_STATE = 128
CONV_WIDTH = 4
CONV_CH = 1536
POOL_WINDOWS = (2, 4, 8, 16)
POOL_GROUP_DIM = 256
POOL_HIST = 15
QK_NOPE = 64
QK_ROPE = 32
V_DIM = 64
MLA_HEADS = 16
Q_LORA = 256
KV_LORA = 256
ROPE_THETA = 10000.0
PEER_HEADS = 8
PEER_NKEYS = 128
PEER_HALF = 128
PEER_TOPK = 16

LANES = 128
BF16_ROWS = 16
VMEM_LIMIT_BYTES = 56 * 1024 * 1024
SSD_BLOCK = 128
HEAD_SLOT = 128

NEG_INF = float("-inf")


def _cparams(*sem):
    return pltpu.CompilerParams(dimension_semantics=sem, vmem_limit_bytes=VMEM_LIMIT_BYTES)


def _dot(a, b):
    return jnp.dot(a, b, preferred_element_type=F32)


def _dot_nt(a, b):
    return lax.dot_general(a, b, (((1,), (1,)), ((), ())), preferred_element_type=F32)


def _split3(x):
    hi = x.astype(BF16)
    r = x - hi.astype(F32)
    mid = r.astype(BF16)
    lo = (r - mid.astype(F32)).astype(BF16)
    return hi, mid, lo


def _sel_rhs(x, sel):
    hi, mid, lo = _split3(x)
    return _dot(hi, sel) + _dot(mid, sel) + _dot(lo, sel)


def _sel_lhs(sel, x):
    hi, mid, lo = _split3(x)
    return _dot(sel, hi) + _dot(sel, mid) + _dot(sel, lo)


def _rms(x, w):
    return x * lax.rsqrt(jnp.mean(x * x, axis=-1, keepdims=True) + EPS) * w


def _silu(x):
    return x * (1.0 / (1.0 + jnp.exp(-x)))


def _softplus(x):
    return jnp.maximum(x, 0.0) + jnp.log(1.0 + jnp.exp(-jnp.abs(x)))


def _full(shape):
    nd = len(shape)
    return pl.BlockSpec(shape, lambda *_: (0,) * nd)


EVEN_COLS = SSD_WIDTH + CONV_CH + SSD_WIDTH + LANES


def _even_in_kernel(h_ref, nw_ref, w_ref, wdt_ref, z_ref, xbc_ref, u_ref, dt_ref, dtt_ref):
    xn = _rms(h_ref[...], nw_ref[...]).astype(BF16)
    proj = _dot(xn, w_ref[...])
    z_ref[...] = proj[:, :SSD_WIDTH]
    xbc_ref[...] = proj[:, SSD_WIDTH:SSD_WIDTH + CONV_CH]
    u_ref[...] = proj[:, SSD_WIDTH + CONV_CH:2 * SSD_WIDTH + CONV_CH]
    dt_ref[...] = proj[:, 2 * SSD_WIDTH + CONV_CH:]
    dtt_ref[...] = _dot_nt(wdt_ref[...], xn)


def _even_in(h, nw, w, wdt, tm):
    t = h.shape[0]
    row = lambda n: pl.BlockSpec((tm, n), lambda i: (i, 0))
    return pl.pallas_call(
        _even_in_kernel,
        grid=(t // tm,),
        in_specs=[row(D_MODEL), _full((1, D_MODEL)), _full((D_MODEL, EVEN_COLS)), _full((SSD_HEADS, D_MODEL))],
        out_specs=[row(SSD_WIDTH), row(CONV_CH), row(SSD_WIDTH), row(LANES),
                   pl.BlockSpec((SSD_HEADS, tm), lambda i: (0, i))],
        out_shape=[jax.ShapeDtypeStruct((t, SSD_WIDTH), F32), jax.ShapeDtypeStruct((t, CONV_CH), F32),
                   jax.ShapeDtypeStruct((t, SSD_WIDTH), F32), jax.ShapeDtypeStruct((t, LANES), F32),
                   jax.ShapeDtypeStruct((SSD_HEADS, t), F32)],
        compiler_params=_cparams("parallel"),
        name="even_in",
    )(h, nw, w, wdt)


def _ssd_kernel(xbc_ref, z_ref, dt_ref, dtt_ref, hist_ref, s0_ref, cw_ref, cb_ref, bias_ref, biast_ref,
                a_ref, at_ref, dskip_ref, nw_ref, expand_ref, y_ref, sfin_ref, ext_ref, st_ref, *, valid_len):
    q = SSD_BLOCK
    c = pl.program_id(1)

    @pl.when(c == 0)
    def _():
        ext_ref[0:8, :] = hist_ref[0]
        st_ref[...] = s0_ref[0]

    ext_ref[8:8 + q, :] = xbc_ref[0]
    conv = cb_ref[...] + cw_ref[0:1, :] * ext_ref[5:5 + q, :]
    for j in range(1, CONV_WIDTH):
        conv = conv + cw_ref[j:j + 1, :] * ext_ref[5 + j:5 + j + q, :]
    ext_ref[0:8, :] = ext_ref[q:q + 8, :]
    xbc = _silu(conv)
    xs = xbc[:, :SSD_WIDTH]
    xs_b = xs.astype(BF16)

    row_i = lax.broadcasted_iota(I32, (q, q), 0)
    col_i = lax.broadcasted_iota(I32, (q, q), 1)
    tril = (row_i >= col_i).astype(BF16)
    triu = (row_i <= col_i).astype(BF16)
    ones = jnp.ones((q, q), BF16)
    tok_ok = (c * q + lax.broadcasted_iota(I32, (q, LANES), 0)) < valid_len
    dt_tok = jnp.where(tok_ok, _softplus(dt_ref[0] + bias_ref[...]), 0.0)
    cs_tok = _sel_lhs(tril, dt_tok * -jnp.exp(a_ref[...]))
    lane_ok = (c * q + lax.broadcasted_iota(I32, (SSD_HEADS, q), 1)) < valid_len
    dt_t = jnp.where(lane_ok, _softplus(dtt_ref[0] + biast_ref[...]), 0.0)
    da_t = dt_t * -jnp.exp(at_ref[...])
    cs_t = _sel_rhs(da_t, triu)
    tot_t = _sel_rhs(da_t, ones)
    w_t = dt_t * jnp.exp(tot_t - cs_t)
    cs_exp = _sel_rhs(cs_tok, expand_ref[...])
    ecs = jnp.exp(cs_exp)
    dchunk = ecs[q - 1:q, :]

    lane_lo = lax.broadcasted_iota(I32, (q, LANES), 1) < SSD_HEAD_DIM
    causal = row_i >= col_i
    r = SSD_HEADS // SSD_GROUPS
    gw = r * SSD_HEAD_DIM
    y_parts = []
    for g in range(SSD_GROUPS):
        bm = xbc[:, SSD_WIDTH + g * SSD_STATE:SSD_WIDTH + (g + 1) * SSD_STATE]
        cm = xbc[:, SSD_WIDTH + (SSD_GROUPS + g) * SSD_STATE:SSD_WIDTH + (SSD_GROUPS + g + 1) * SSD_STATE]
        cm_b = cm.astype(BF16)
        cb = _dot_nt(cm_b, bm.astype(BF16))
        bm_t = bm.T
        st_g = st_ref[g]
        y_off = _dot(cm_b, st_g.astype(BF16)) * ecs[:, g * gw:(g + 1) * gw]
        yd, up = [], []
        for pr in range(r // 2):
            res_y, res_u = [], []
            xs_pair = xs_b[:, g * gw + pr * LANES:g * gw + (pr + 1) * LANES]
            for k in range(2):
                hh = g * r + 2 * pr + k
                seg = cs_tok[:, hh:hh + 1] - cs_t[hh:hh + 1, :]
                lmat = jnp.where(causal, jnp.exp(seg), 0.0)
                m = (cb * lmat * dt_t[hh:hh + 1, :]).astype(BF16)
                res_y.append(_dot(m, xs_pair))
                res_u.append(_dot((bm_t * w_t[hh:hh + 1, :]).astype(BF16), xs_pair))
            yd.append(jnp.where(lane_lo, res_y[0], res_y[1]))
            up.append(jnp.where(lane_lo, res_u[0], res_u[1]))
        y_parts.append(jnp.concatenate(yd, axis=1) + y_off)
        st_ref[g] = dchunk[:, g * gw:(g + 1) * gw] * st_g + jnp.concatenate(up, axis=1)

    y = jnp.concatenate(y_parts, axis=1) + dskip_ref[...] * xs
    y = y * _silu(z_ref[0])
    outs = []
    for g in range(SSD_GROUPS):
        yg = y[:, g * gw:(g + 1) * gw]
        outs.append(yg * lax.rsqrt(jnp.mean(yg * yg, axis=-1, keepdims=True) + EPS))
    y_ref[0] = (jnp.concatenate(outs, axis=1) * nw_ref[...]).astype(y_ref.dtype)

    @pl.when(c == pl.num_programs(1) - 1)
    def _():
        sfin_ref[0] = st_ref[...]


def _ssd(xbc, z, dt, dtt, hist, s0, p, valid_len):
    b, l, _ = xbc.shape
    q = SSD_BLOCK
    n_state = SSD_GROUPS * SSD_STATE
    gw = SSD_WIDTH // SSD_GROUPS
    blk = lambda n: pl.BlockSpec((1, q, n), lambda i, c: (i, c, 0))
    per_b = lambda s: pl.BlockSpec((1,) + s, lambda i, c: (i,) + (0,) * len(s))
    return pl.pallas_call(
        functools.partial(_ssd_kernel, valid_len=valid_len),
        grid=(b, l // q),
        in_specs=[blk(CONV_CH), blk(SSD_WIDTH), blk(LANES),
                  pl.BlockSpec((1, SSD_HEADS, q), lambda i, c: (i, 0, c)),
                  per_b((8, CONV_CH)), per_b((SSD_GROUPS, SSD_STATE, gw)),
                  _full((CONV_WIDTH, CONV_CH)), _full((1, CONV_CH)), _full((1, LANES)), _full((SSD_HEADS, q)),
                  _full((1, LANES)), _full((SSD_HEADS, q)), _full((1, SSD_WIDTH)), _full((1, SSD_WIDTH)),
                  _full((LANES, SSD_WIDTH))],
        out_specs=[blk(SSD_WIDTH), per_b((SSD_GROUPS, SSD_STATE, gw))],
        out_shape=[jax.ShapeDtypeStruct((b, l, SSD_WIDTH), BF16),
                   jax.ShapeDtypeStruct((b, SSD_GROUPS, SSD_STATE, gw), F32)],
        scratch_shapes=[pltpu.VMEM((q + 8, CONV_CH), F32), pltpu.VMEM((SSD_GROUPS, SSD_STATE, gw), F32)],
        compiler_params=_cparams("parallel", "arbitrary"),
        name="ssd_scan",
    )(xbc, z, dt, dtt, hist, s0, p["conv_w"], p["conv_b"], p["dt_bias"], p["dt_bias_t"], p["a_log"], p["a_log_t"],
      p["d_skip"], p["ssd_norm_w"], p["expand"])


def _pool_out_kernel(u_ref, hist_ref, y_ref, h_ref, pw_ref, ps_ref, wo_ref, o_ref, ext_ref, *, pos0, tm):
    i = pl.program_id(1)

    @pl.when(i == 0)
    def _():
        ext_ref[0:16, :] = hist_ref[0]

    ext_ref[16:16 + tm, :] = u_ref[0]
    pos = pos0 + i * tm + lax.broadcasted_iota(I32, (tm, POOL_GROUP_DIM), 0)
    acc = h_ref[0] + _dot(y_ref[0], wo_ref[0:SSD_WIDTH, :])
    yps = []
    for gi, wsz in enumerate(POOL_WINDOWS):
        c0 = gi * POOL_GROUP_DIM
        cur = ext_ref[16:16 + tm, c0:c0 + POOL_GROUP_DIM]
        tot = cur
        for j in range(1, wsz):
            tot = tot + ext_ref[16 - j:16 - j + tm, c0:c0 + POOL_GROUP_DIM]
        cnt = jnp.minimum(pos + 1, wsz).astype(F32)
        pooled = tot / cnt - cur
        yps.append(_dot(pooled.astype(BF16), pw_ref[gi]))
    yp = (jnp.concatenate(yps, axis=1) * ps_ref[...]).astype(BF16)
    o_ref[0] = acc + _dot(yp, wo_ref[SSD_WIDTH:, :])
    ext_ref[0:16, :] = ext_ref[tm:tm + 16, :]


def _pool_out(u, hist, y, h, p, pos0, tm):
    b, l, _ = u.shape
    blk = lambda n: pl.BlockSpec((1, tm, n), lambda bi, i: (bi, i, 0))
    return pl.pallas_call(
        functools.partial(_pool_out_kernel, pos0=pos0, tm=tm),
        grid=(b, l // tm),
        in_specs=[blk(SSD_WIDTH), pl.BlockSpec((1, 16, SSD_WIDTH), lambda bi, i: (bi, 0, 0)), blk(SSD_WIDTH),
                  blk(D_MODEL), _full((len(POOL_WINDOWS), POOL_GROUP_DIM, POOL_GROUP_DIM)), _full((1, SSD_WIDTH)),
                  _full((2 * SSD_WIDTH, D_MODEL))],
        out_specs=blk(D_MODEL),
        out_shape=jax.ShapeDtypeStruct((b, l, D_MODEL), F32),
        scratch_shapes=[pltpu.VMEM((tm + 16, SSD_WIDTH), F32)],
        compiler_params=_cparams("parallel", "arbitrary"),
        name="pool_out",
    )(u, hist, y, h, p["pool_w"], p["pool_scale"], p["w_out"])


ODD_COLS = Q_LORA + KV_LORA + 2 * LANES
QK_ROWS = MLA_HEADS * HEAD_SLOT


def _odd_in_kernel(h_ref, nw_ref, w_ref, qn_ref, kvn_ref, wa_ref, wb_ref, cos_ref, sin_ref, r128_ref, rq_ref,
                   rqm_ref, ckv_ref, kpe_ref, qt_ref, *, scale):
    xn = _rms(h_ref[...], nw_ref[...]).astype(BF16)
    proj = _dot(xn, w_ref[...])
    cq = _rms(proj[:, :Q_LORA], qn_ref[...]).astype(BF16)
    ckv_ref[...] = _rms(proj[:, Q_LORA:Q_LORA + KV_LORA], kvn_ref[...])
    cos = cos_ref[...]
    sin = sin_ref[...]
    kpe = proj[:, Q_LORA + KV_LORA:Q_LORA + KV_LORA + LANES]
    kpe_sw = proj[:, Q_LORA + KV_LORA + LANES:]
    kpe_rot = kpe * _sel_rhs(cos, r128_ref[...]) + kpe_sw * _sel_rhs(sin, r128_ref[...])
    kpe_ref[...] = kpe_rot[:, :QK_ROPE]
    cos_q = (_sel_rhs(cos, rq_ref[...]) + rqm_ref[...]) * scale
    sin_q = _sel_rhs(sin, rq_ref[...]) * scale
    q = _dot(cq, wa_ref[...]) * cos_q + _dot(cq, wb_ref[...]) * sin_q
    qt_ref[...] = q.T.astype(qt_ref.dtype)


def _odd_in(h, cos, sin, p, tm):
    t = h.shape[0]
    row = lambda n: pl.BlockSpec((tm, n), lambda i: (i, 0))
    half = QK_ROPE // 2
    scale = float((QK_NOPE + QK_ROPE) ** -0.5 * np.log2(np.e))
    return pl.pallas_call(
        functools.partial(_odd_in_kernel, scale=scale),
        grid=(t // tm,),
        in_specs=[row(D_MODEL), _full((1, D_MODEL)), _full((D_MODEL, ODD_COLS)), _full((1, Q_LORA)),
                  _full((1, KV_LORA)), _full((Q_LORA, QK_ROWS)), _full((Q_LORA, QK_ROWS)), row(half), row(half),
                  _full((half, LANES)), _full((half, QK_ROWS)), _full((1, QK_ROWS))],
        out_specs=[row(KV_LORA), row(QK_ROPE), pl.BlockSpec((QK_ROWS, tm), lambda i: (0, i))],
        out_shape=[jax.ShapeDtypeStruct((t, KV_LORA), F32), jax.ShapeDtypeStruct((t, QK_ROPE), F32),
                   jax.ShapeDtypeStruct((QK_ROWS, t), BF16)],
        compiler_params=_cparams("parallel"),
        name="odd_in",
    )(h, p["norm_mix"], p["w_in"], p["q_norm"], p["kv_norm"], p["w_qa"], p["w_qb"], cos, sin, p["r128"], p["rq"],
      p["rq_mask"])


def _kv_expand_kernel(ckv_ref, kpe_ref, wk_ref, rk_ref, wvt_ref, k_ref, vt_ref):
    ckv = ckv_ref[0].astype(BF16)
    k_ref[0] = (_dot(ckv, wk_ref[...]) + _dot(kpe_ref[0].astype(BF16), rk_ref[...])).astype(k_ref.dtype)
    vt_ref[0] = _dot_nt(wvt_ref[...], ckv).astype(vt_ref.dtype)


def _kv_expand(ckv, kpe, p, tm):
    b, t, _ = ckv.shape
    vrows = MLA_HEADS * V_DIM
    return pl.pallas_call(
        _kv_expand_kernel,
        grid=(b, t // tm),
        in_specs=[pl.BlockSpec((1, tm, KV_LORA), lambda bi, i: (bi, i, 0)),
                  pl.BlockSpec((1, tm, QK_ROPE), lambda bi, i: (bi, i, 0)),
                  _full((KV_LORA, QK_ROWS)), _full((QK_ROPE, QK_ROWS)), _full((vrows, KV_LORA))],
        out_specs=[pl.BlockSpec((1, tm, QK_ROWS), lambda bi, i: (bi, i, 0)),
                   pl.BlockSpec((1, vrows, tm), lambda bi, i: (bi, 0, i))],
        out_shape=[jax.ShapeDtypeStruct((b, t, QK_ROWS), BF16), jax.ShapeDtypeStruct((b, vrows, t), BF16)],
        compiler_params=_cparams("parallel", "parallel"),
        name="kv_expand",
    )(ckv, kpe, p["w_k"], p["r_k"], p["w_vt"])


ATTN_HEADS_PER_STEP = 4
ATTN_SUB = 256
ATTN_LOOKAHEAD = 5


def _attn_kernel(qi_ref, ki_ref, qt_ref, k_ref, vt_ref, o_ref, m_ref, l_ref, acc_ref, *, tq, tk, causal,
                 kv_valid):
    s_idx = pl.program_id(2)
    qi = qi_ref[s_idx]
    ki = ki_ref[s_idx]
    nh = ATTN_HEADS_PER_STEP

    @pl.when(ki == 0)
    def _():
        m_ref[...] = jnp.full(m_ref.shape, NEG_INF, F32)
        l_ref[...] = jnp.zeros(l_ref.shape, F32)
        acc_ref[...] = jnp.zeros(acc_ref.shape, F32)

    sk, sq = min(ATTN_SUB, tk), min(ATTN_SUB, tq)

    def step(diagonal):
        items = []
        for ks in range(tk // sk):
            for hh in range(nh):
                for qs in range(tq // sq):
                    if causal and diagonal and ks * sk >= (qs + 1) * sq:
                        continue
                    if not causal and kv_valid is not None and ks * sk >= kv_valid:
                        continue
                    items.append((ks, hh, qs))

        def needs_mask(ks, qs):
            if causal:
                return diagonal and (ks + 1) * sk > qs * sq
            return kv_valid is not None and (ks + 1) * sk > kv_valid

        def scores(ks, hh, qs):
            return _dot(k_ref[0, ks * sk:(ks + 1) * sk, hh * HEAD_SLOT:(hh + 1) * HEAD_SLOT],
                        qt_ref[0, hh * HEAD_SLOT:(hh + 1) * HEAD_SLOT, qs * sq:(qs + 1) * sq])

        state = {}
        for hh in range(nh):
            for qs in range(tq // sq):
                cols = slice(qs * sq, (qs + 1) * sq)
                state[hh, qs] = (m_ref[hh:hh + 1, cols], l_ref[hh:hh + 1, cols],
                                 acc_ref[hh * V_DIM:(hh + 1) * V_DIM, cols])
        pending = [scores(*it) for it in items[:ATTN_LOOKAHEAD]]
        for idx, (ks, hh, qs) in enumerate(items):
            st = pending.pop(0)
            if idx + ATTN_LOOKAHEAD < len(items):
                pending.append(scores(*items[idx + ATTN_LOOKAHEAD]))
            if needs_mask(ks, qs):
                kpos = ks * sk + lax.broadcasted_iota(I32, (sk, sq), 0)
                if causal:
                    qpos = qs * sq + lax.broadcasted_iota(I32, (sk, sq), 1)
                    ok = (kpos // CHUNK) <= (qpos // CHUNK)
                else:
                    ok = kpos < kv_valid
                st = jnp.where(ok, st, NEG_INF)
            m_prev, l_prev, acc_prev = state[hh, qs]
            m_new = jnp.maximum(m_prev, jnp.max(st, axis=0, keepdims=True))
            alpha = jnp.exp2(m_prev - m_new)
            pt = jnp.exp2(st - m_new)
            l_new = alpha * l_prev + jnp.sum(pt, axis=0, keepdims=True)
            pv = _dot(vt_ref[0, hh * V_DIM:(hh + 1) * V_DIM, ks * sk:(ks + 1) * sk], pt.astype(BF16))
            state[hh, qs] = (m_new, l_new, alpha * acc_prev + pv)
        for (hh, qs), (m_new, l_new, acc_new) in state.items():
            cols = slice(qs * sq, (qs + 1) * sq)
            m_ref[hh:hh + 1, cols] = m_new
            l_ref[hh:hh + 1, cols] = l_new
            acc_ref[hh * V_DIM:(hh + 1) * V_DIM, cols] = acc_new

    def finish():
        for hh in range(nh):
            vrows = slice(hh * V_DIM, (hh + 1) * V_DIM)
            o_ref[0, vrows, :] = (acc_ref[vrows, :] / l_ref[hh:hh + 1, :]).astype(o_ref.dtype)

    if causal:
        last = ((qi + 1) * tq - 1) // tk

        @pl.when(ki < last)
        def _():
            step(False)

        @pl.when(ki == last)
        def _():
            step(True)
            finish()
    else:
        step(False)
        finish()


def _attention(qt, k, vt, tq, tk, causal, kv_valid):
    b, _, t_q = qt.shape
    t_k = k.shape[1]
    nq, nk = t_q // tq, t_k // tk
    nh = ATTN_HEADS_PER_STEP
    if causal:
        assert tk == tq
        pairs = [(a, c) for a in range(nq) for c in range(((a + 1) * tq - 1) // tk + 1)]
    else:
        assert nk == 1
        pairs = [(a, 0) for a in range(nq)]
    qi = jnp.asarray(np.array([a for a, _ in pairs], np.int32))
    ki = jnp.asarray(np.array([c for _, c in pairs], np.int32))
    grid_spec = pltpu.PrefetchScalarGridSpec(
        num_scalar_prefetch=2,
        grid=(b, MLA_HEADS // nh, len(pairs)),
        in_specs=[pl.BlockSpec((1, nh * HEAD_SLOT, tq), lambda bi, hi, s, qi_r, ki_r: (bi, hi, qi_r[s])),
                  pl.BlockSpec((1, tk, nh * HEAD_SLOT), lambda bi, hi, s, qi_r, ki_r: (bi, ki_r[s], hi)),
                  pl.BlockSpec((1, nh * V_DIM, tk), lambda bi, hi, s, qi_r, ki_r: (bi, hi, ki_r[s]))],
        out_specs=pl.BlockSpec((1, nh * V_DIM, tq), lambda bi, hi, s, qi_r, ki_r: (bi, hi, qi_r[s])),
        scratch_shapes=[pltpu.VMEM((nh, tq), F32), pltpu.VMEM((nh, tq), F32), pltpu.VMEM((nh * V_DIM, tq), F32)],
    )
    return pl.pallas_call(
        functools.partial(_attn_kernel, tq=tq, tk=tk, causal=causal, kv_valid=kv_valid),
        grid_spec=grid_spec,
        out_shape=jax.ShapeDtypeStruct((b, MLA_HEADS * V_DIM, t_q), F32),
        compiler_params=_cparams("parallel", "parallel", "arbitrary"),
        name="attention",
    )(qi, ki, qt, k, vt)


def _attn_out_kernel(ot_ref, h_ref, wo_ref, o_ref):
    o_ref[...] = h_ref[...] + _dot(ot_ref[...].T.astype(BF16), wo_ref[...])


def _attn_out(ot, h, wo, tm):
    t = h.shape[0]
    return pl.pallas_call(
        _attn_out_kernel,
        grid=(t // tm,),
        in_specs=[pl.BlockSpec((MLA_HEADS * V_DIM, tm), lambda i: (0, i)),
                  pl.BlockSpec((tm, D_MODEL), lambda i: (i, 0)), _full((MLA_HEADS * V_DIM, D_MODEL))],
        out_specs=pl.BlockSpec((tm, D_MODEL), lambda i: (i, 0)),
        out_shape=jax.ShapeDtypeStruct((t, D_MODEL), F32),
        compiler_params=_cparams("parallel"),
        name="attn_out",
    )(ot, h, wo)


_STAIR = [(a, b) for a in range(PEER_TOPK) for b in range(PEER_TOPK) if (a + 1) * (b + 1) <= PEER_TOPK]
STAIR_ROWS = 56
NOT_RANKED = PEER_TOPK
PAD_FLAT_INDEX = PEER_TOPK * PEER_TOPK


def _top_values(s):
    vals = []
    x = s
    for _ in range(PEER_TOPK):
        m = jnp.max(x, axis=0, keepdims=True)
        x = jnp.where(x == m, NEG_INF, x)
        vals.append(m)
    return jnp.concatenate(vals, axis=0)


def _count_ge(x, thr):
    return jnp.sum(jnp.where(x >= thr, 1.0, 0.0), axis=0, keepdims=True)


def _route_by_value(s1, s2, sel_a, sel_b, sel_at, fidx):
    v1 = _top_values(s1)
    v2 = []
    kb = jnp.full(s2.shape, float(NOT_RANKED), F32)
    x = s2
    for b in range(PEER_TOPK):
        m = jnp.max(x, axis=0, keepdims=True)
        hit = x == m
        kb = jnp.where(hit, float(b), kb)
        x = jnp.where(hit, NEG_INF, x)
        v2.append(m)
    v2 = jnp.concatenate(v2, axis=0)
    cand = _sel_lhs(sel_a, v1) + _sel_lhs(sel_b, v2)
    cand = jnp.where(fidx < PAD_FLAT_INDEX, cand, NEG_INF)
    tops = _top_values(cand)
    zsum = jnp.sum(jnp.exp(tops - tops[0:1, :]), axis=0, keepdims=True)
    picked = cand >= tops[PEER_TOPK - 1:PEER_TOPK, :]
    cnt = _dot(sel_at, jnp.where(picked, 1.0, 0.0).astype(BF16))
    th = jnp.zeros(s1.shape, F32)
    for a in range(PEER_TOPK):
        th = jnp.where(s1 == v1[a:a + 1, :], cnt[a:a + 1, :], th)
    in1 = s1 >= v1[PEER_TOPK - 1:PEER_TOPK, :]
    in2 = s2 >= v2[PEER_TOPK - 1:PEER_TOPK, :]
    e1 = jnp.where(in1, jnp.exp(s1 - v1[0:1, :]), 0.0) * (GELU_HALF / zsum)
    e2 = jnp.where(in2, jnp.exp(s2 - v2[0:1, :]), 0.0)
    k = float(PEER_TOPK)
    clean = ((jnp.sum(jnp.where(in1, 1.0, 0.0), axis=0, keepdims=True) == k)
             & (jnp.sum(jnp.where(in2, 1.0, 0.0), axis=0, keepdims=True) == k)
             & (jnp.sum(jnp.where(picked, 1.0, 0.0), axis=0, keepdims=True) == k))
    tie = jnp.max(jnp.where(clean, 0.0, 1.0))
    return e1, th, e2, kb, tie


def _topk_columns(s, key_iota):
    rank = jnp.full(s.shape, NOT_RANKED, I32)
    vals = []
    x = s
    for a in range(PEER_TOPK):
        m = jnp.max(x, axis=0, keepdims=True)
        first = jnp.min(jnp.where(x == m, key_iota, PEER_NKEYS), axis=0, keepdims=True)
        hit = key_iota == first
        rank = jnp.where(hit, a, rank)
        x = jnp.where(hit, NEG_INF, x)
        vals.append(m)
    return jnp.concatenate(vals, axis=0), rank


def _route_by_rank(s1, s2, sel_a, sel_b, fidx):
    tm = s1.shape[1]
    key_iota = lax.broadcasted_iota(I32, (PEER_NKEYS, tm), 0)
    a_iota = lax.broadcasted_iota(I32, (PEER_TOPK, tm), 0)
    v1, r1 = _topk_columns(s1, key_iota)
    v2, r2 = _topk_columns(s2, key_iota)
    cand = _sel_lhs(sel_a, v1) + _sel_lhs(sel_b, v2)
    cand = jnp.where(fidx < PAD_FLAT_INDEX, cand, NEG_INF)
    top = None
    zsum = None
    cnt = jnp.zeros((PEER_TOPK, tm), F32)
    for k in range(PEER_TOPK):
        m = jnp.max(cand, axis=0, keepdims=True)
        f = jnp.min(jnp.where(cand == m, fidx, PAD_FLAT_INDEX), axis=0, keepdims=True)
        cand = jnp.where(fidx == f, NEG_INF, cand)
        if k == 0:
            top = m
            zsum = jnp.ones_like(m)
        else:
            zsum = zsum + jnp.exp(m - top)
        cnt = jnp.where(a_iota == (f >> 4), cnt + 1.0, cnt)
    th = jnp.zeros(s1.shape, F32)
    for a in range(PEER_TOPK):
        th = jnp.where(r1 == a, cnt[a:a + 1, :], th)
    e1 = jnp.where(r1 < NOT_RANKED, jnp.exp(s1 - v1[0:1, :]), 0.0) * (GELU_HALF / zsum)
    e2 = jnp.where(r2 < NOT_RANKED, jnp.exp(s2 - v2[0:1, :]), 0.0)
    return e1, th, e2, r2.astype(F32)


def _peer_route_kernel(h_ref, nw_ref, wq_ref, kh_ref, kl_ref, sela_ref, selb_ref, selat_ref, fidx_ref,
                       xn_ref, e1_ref, th_ref, e2_ref, kb_ref, qt_ref):
    xn = _rms(h_ref[...], nw_ref[...]).astype(BF16)
    xn_ref[...] = xn
    qt_ref[...] = _dot_nt(wq_ref[...], xn)

    def one_head(hd, carry):
        scores = []
        for c in range(2):
            hc = 2 * hd + c
            qs = qt_ref[pl.ds(pl.multiple_of(hc * PEER_HALF, PEER_HALF), PEER_HALF), :]
            q_hi = qs.astype(BF16)
            q_lo = (qs - q_hi.astype(F32)).astype(BF16)
            scores.append(_dot(kh_ref[hc], q_hi) + _dot(kh_ref[hc], q_lo) + _dot(kl_ref[hc], q_hi))
        e1, th, e2, kb, tie = _route_by_value(scores[0], scores[1], sela_ref[...], selb_ref[...], selat_ref[...],
                                              fidx_ref[...])
        e1_ref[hd] = e1
        th_ref[hd] = th
        e2_ref[hd] = e2.astype(e2_ref.dtype)
        kb_ref[hd] = kb.astype(kb_ref.dtype)

        @pl.when(tie > 0.0)
        def _():
            e1x, thx, e2x, kbx = _route_by_rank(scores[0], scores[1], sela_ref[...], selb_ref[...], fidx_ref[...])
            e1_ref[hd] = e1x
            th_ref[hd] = thx
            e2_ref[hd] = e2x.astype(e2_ref.dtype)
            kb_ref[hd] = kbx.astype(kb_ref.dtype)

        return carry

    lax.fori_loop(0, PEER_HEADS, one_head, 0)


def _peer_route(h, p, tm):
    t = h.shape[0]
    tab = pl.BlockSpec((PEER_HEADS, PEER_NKEYS, tm), lambda i: (0, 0, i))
    tab_shape = jax.ShapeDtypeStruct((PEER_HEADS, PEER_NKEYS, t), F32)
    tab_shape_b = jax.ShapeDtypeStruct((PEER_HEADS, PEER_NKEYS, t), BF16)
    fidx = jnp.asarray(np.tile(np.array([a * PEER_TOPK + b for a, b in _STAIR]
                                        + [PAD_FLAT_INDEX] * (STAIR_ROWS - len(_STAIR)), np.int32)[:, None], (1, tm)))
    n_q = 2 * PEER_HEADS * PEER_HALF
    return pl.pallas_call(
        _peer_route_kernel,
        grid=(t // tm,),
        in_specs=[pl.BlockSpec((tm, D_MODEL), lambda i: (i, 0)), _full((1, D_MODEL)),
                  _full((n_q, D_MODEL)),
                  _full((2 * PEER_HEADS, PEER_NKEYS, PEER_HALF)), _full((2 * PEER_HEADS, PEER_NKEYS, PEER_HALF)),
                  _full((STAIR_ROWS, PEER_TOPK)), _full((STAIR_ROWS, PEER_TOPK)), _full((PEER_TOPK, STAIR_ROWS)),
                  _full((STAIR_ROWS, tm))],
        out_specs=[pl.BlockSpec((tm, D_MODEL), lambda i: (i, 0)), tab, tab, tab, tab],
        out_shape=[jax.ShapeDtypeStruct((t, D_MODEL), BF16), tab_shape, tab_shape, tab_shape_b, tab_shape_b],
        scratch_shapes=[pltpu.VMEM((n_q, tm), F32)],
        compiler_params=_cparams("parallel"),
        name="peer_route",
    )(h, p["norm_ffn"], p["wq_t"], p["keys_hi"], p["keys_lo"], p["sel_a"], p["sel_b"], p["sel_at"], fidx)


EXPERT_BLOCK = 1024
ROWS_PER_BLOCK = EXPERT_BLOCK // PEER_NKEYS
MIX_GROUP = 512
GELU_HALF = 0.5
MIX_PIECE_SLABS = 2


def _gelu_tanh_doubled(x):
    return x * (1.0 + jnp.tanh(0.7978845608028654 * (x + 0.044715 * (x * x * x))))


def _peer_mix_kernel(h_ref, xn_ref, e1_rows_ref, th_rows_ref, e2_ref, kb_ref, u_ref, vt_ref, o_ref, wg_ref, acc_ref,
                     ht_ref, e1_ref, th_ref):
    j = pl.program_id(1)

    @pl.when(j == 0)
    def _():
        acc_ref[...] = jnp.zeros(acc_ref.shape, F32)

    tm = xn_ref.shape[0]
    n_groups = EXPERT_BLOCK // MIX_GROUP
    keys_per_group = MIX_GROUP // PEER_NKEYS
    piece_rows = MIX_PIECE_SLABS * BF16_ROWS
    for hd in range(PEER_HEADS):
        for r in range(ROWS_PER_BLOCK):
            th_ref[hd, r] = jnp.broadcast_to(th_rows_ref[hd, r:r + 1, :], (BF16_ROWS, tm)).astype(BF16)
            e1_ref[hd, r] = jnp.broadcast_to(e1_rows_ref[hd, r:r + 1, :], (BF16_ROWS, tm)).astype(BF16)

    def expert_inputs(g):
        return _dot_nt(u_ref[g * MIX_GROUP:(g + 1) * MIX_GROUP, :], xn_ref[...])

    ht_ref[0] = expert_inputs(0)
    for g in range(n_groups):
        ht_cur = ht_ref.at[g % 2]
        if g + 1 < n_groups:
            ht_ref[(g + 1) % 2] = expert_inputs(g + 1)
        for rr in range(keys_per_group):
            r = g * keys_per_group + rr
            for q in range(PEER_NKEYS // piece_rows):
                slabs = slice(q * MIX_PIECE_SLABS, (q + 1) * MIX_PIECE_SLABS)
                w = None
                for hd in range(PEER_HEADS):
                    hit = kb_ref[hd, slabs] < th_ref[hd, r][None]
                    term = jnp.where(hit, e1_ref[hd, r][None] * e2_ref[hd, slabs], jnp.zeros((), BF16))
                    w = term if w is None else w + term
                lo = rr * PEER_NKEYS + q * piece_rows
                act = _gelu_tanh_doubled(ht_cur[lo:lo + piece_rows, :]).astype(BF16)
                lo = r * PEER_NKEYS + q * piece_rows
                wg_ref[lo:lo + piece_rows, :] = w.reshape(piece_rows, tm) * act
    acc_ref[...] += _dot(vt_ref[...], wg_ref[...])

    @pl.when(j == pl.num_programs(1) - 1)
    def _():
        o_ref[...] = h_ref[...] + acc_ref[...].T


def _peer_mix(h, xn, e1, th, e2, kb, u, vt, tm):
    t = h.shape[0]
    n_exp = u.shape[0]
    rowblk = pl.BlockSpec((PEER_HEADS, ROWS_PER_BLOCK, tm), lambda i, j: (0, j, i))
    n_slabs = PEER_NKEYS // BF16_ROWS
    fullblk = pl.BlockSpec((PEER_HEADS, n_slabs, BF16_ROWS, tm), lambda i, j: (0, 0, 0, i))
    e2 = e2.reshape(PEER_HEADS, n_slabs, BF16_ROWS, t)
    kb = kb.reshape(PEER_HEADS, n_slabs, BF16_ROWS, t)
    row_scratch = pltpu.VMEM((PEER_HEADS, ROWS_PER_BLOCK, BF16_ROWS, tm), BF16)
    return pl.pallas_call(
        _peer_mix_kernel,
        grid=(t // tm, n_exp // EXPERT_BLOCK),
        in_specs=[pl.BlockSpec((tm, D_MODEL), lambda i, j: (i, 0)), pl.BlockSpec((tm, D_MODEL), lambda i, j: (i, 0)),
                  rowblk, rowblk, fullblk, fullblk,
                  pl.BlockSpec((EXPERT_BLOCK, D_MODEL), lambda i, j: (j, 0)),
                  pl.BlockSpec((D_MODEL, EXPERT_BLOCK), lambda i, j: (0, j))],
        out_specs=pl.BlockSpec((tm, D_MODEL), lambda i, j: (i, 0)),
        out_shape=jax.ShapeDtypeStruct((t, D_MODEL), F32),
        scratch_shapes=[pltpu.VMEM((EXPERT_BLOCK, tm), BF16), pltpu.VMEM((D_MODEL, tm), F32),
                        pltpu.VMEM((2, MIX_GROUP, tm), F32), row_scratch, row_scratch],
        compiler_params=_cparams("parallel", "arbitrary"),
        name="peer_mix",
    )(h, xn, e1, th, e2, kb, u, vt)


def _ple_kernel(h_ref, p_ref, nw_ref, wg_ref, wp_ref, fnw_ref, o_ref, *, final):
    h = h_ref[...]
    gate = 1.0 / (1.0 + jnp.exp(-_dot(_rms(h, nw_ref[...]).astype(BF16), wg_ref[...])))
    out = h + gate * _dot(p_ref[...].astype(BF16), wp_ref[...])
    if final:
        out = _rms(out, fnw_ref[...])
    o_ref[...] = out


def _ple(h, pemb, p, fnw, final, tm):
    t = h.shape[0]
    pd = pemb.shape[1]
    return pl.pallas_call(
        functools.partial(_ple_kernel, final=final),
        grid=(t // tm,),
        in_specs=[pl.BlockSpec((tm, D_MODEL), lambda i: (i, 0)), pl.BlockSpec((tm, pd), lambda i: (i, 0)),
                  _full((1, D_MODEL)), _full((D_MODEL, D_MODEL)), _full((pd, D_MODEL)), _full((1, D_MODEL))],
        out_specs=pl.BlockSpec((tm, D_MODEL), lambda i: (i, 0)),
        out_shape=jax.ShapeDtypeStruct((t, D_MODEL), F32),
        compiler_params=_cparams("parallel"),
        name="ple",
    )(h, pemb, p["ple_norm"], p["w_ple_gate"], p["w_ple_proj"], fnw)


def _row(v, width=None):
    v = v.reshape(1, -1).astype(F32)
    if width is not None and v.shape[1] < width:
        v = jnp.pad(v, ((0, 0), (0, width - v.shape[1])))
    return v


def _prep_even(w, e):
    win = w["w_in_e"][e]
    c0, c1, c2 = SSD_WIDTH, SSD_WIDTH + CONV_CH, SSD_WIDTH + CONV_CH + SSD_HEADS
    w_cat = jnp.concatenate([win[:, :c0], win[:, c0:c1], win[:, c2:], win[:, c1:c2],
                             jnp.zeros((D_MODEL, LANES - SSD_HEADS), F32)], axis=1).astype(BF16)
    expand = np.zeros((LANES, SSD_WIDTH), np.float32)
    for hh in range(SSD_HEADS):
        expand[hh, hh * SSD_HEAD_DIM:(hh + 1) * SSD_HEAD_DIM] = 1.0
    return dict(
        w_in=w_cat, w_dt_t=win[:, c1:c2].T.astype(BF16),
        conv_w=w["conv_w"][e], conv_b=_row(w["conv_b"][e]),
        dt_bias=_row(w["dt_bias"][e], LANES),
        dt_bias_t=jnp.broadcast_to(w["dt_bias"][e][:, None], (SSD_HEADS, SSD_BLOCK)),
        a_log=_row(w["a_log"][e], LANES),
        a_log_t=jnp.broadcast_to(w["a_log"][e][:, None], (SSD_HEADS, SSD_BLOCK)),
        d_skip=_row(jnp.repeat(w["d_skip"][e], SSD_HEAD_DIM)), ssd_norm_w=_row(w["ssd_norm_w"][e]),
        expand=jnp.asarray(expand, BF16),
        pool_w=w["pool_w"][e].astype(BF16), pool_scale=_row(w["pool_scale"][e]),
        w_out=w["w_out_e"][e].astype(BF16))


def _prep_odd(w, o):
    half = QK_ROPE // 2
    win = w["w_in_o"][o]
    w_kpe = win[:, Q_LORA + KV_LORA:]
    w_kpe_sw = jnp.concatenate([-w_kpe[:, half:], w_kpe[:, :half]], axis=1)
    zpad = jnp.zeros((D_MODEL, LANES - QK_ROPE), F32)
    w_cat = jnp.concatenate([win[:, :Q_LORA + KV_LORA], w_kpe, zpad, w_kpe_sw, zpad], axis=1).astype(BF16)
    wuq = w["w_uq"][o].reshape(Q_LORA, MLA_HEADS, QK_NOPE + QK_ROPE)
    nope, pe = wuq[..., :QK_NOPE], wuq[..., QK_NOPE:]
    pe_sw = jnp.concatenate([-pe[..., half:], pe[..., :half]], axis=-1)
    z32 = jnp.zeros((Q_LORA, MLA_HEADS, HEAD_SLOT - QK_NOPE - QK_ROPE), F32)
    w_qa = jnp.concatenate([nope, pe, z32], axis=-1).reshape(Q_LORA, QK_ROWS).astype(BF16)
    w_qb = jnp.concatenate([jnp.zeros_like(nope), pe_sw, z32], axis=-1).reshape(Q_LORA, QK_ROWS).astype(BF16)
    wukv = w["w_ukv"][o].reshape(KV_LORA, MLA_HEADS, QK_NOPE + V_DIM)
    w_k = jnp.concatenate([wukv[..., :QK_NOPE], jnp.zeros((KV_LORA, MLA_HEADS, HEAD_SLOT - QK_NOPE), F32)],
                          axis=-1).reshape(KV_LORA, QK_ROWS).astype(BF16)
    w_vt = wukv[..., QK_NOPE:].reshape(KV_LORA, MLA_HEADS * V_DIM).T.astype(BF16)
    r128 = np.zeros((half, LANES), np.float32)
    rq = np.zeros((half, QK_ROWS), np.float32)
    rq_mask = np.zeros((1, QK_ROWS), np.float32)
    r_k = np.zeros((QK_ROPE, QK_ROWS), np.float32)
    for i in range(half):
        r128[i, i] = r128[i, half + i] = 1.0
    for hh in range(MLA_HEADS):
        base = hh * HEAD_SLOT
        rq_mask[0, base:base + QK_NOPE] = 1.0
        for i in range(half):
            rq[i, base + QK_NOPE + i] = rq[i, base + QK_NOPE + half + i] = 1.0
        for i in range(QK_ROPE):
            r_k[i, base + QK_NOPE + i] = 1.0
    return dict(
        w_in=w_cat, q_norm=_row(w["q_norm"][o]), kv_norm=_row(w["kv_norm"][o]), w_qa=w_qa, w_qb=w_qb,
        r128=jnp.asarray(r128, BF16), rq=jnp.asarray(rq, BF16), rq_mask=jnp.asarray(rq_mask),
        w_k=w_k, r_k=jnp.asarray(r_k, BF16), w_vt=w_vt, w_out=w["w_out_o"][o].astype(BF16))


def _hi_lo(x):
    hi = x.astype(BF16)
    return hi, (x - hi.astype(F32)).astype(BF16)


def _prep_common(w, i):
    keys_hi, keys_lo = _hi_lo(w["peer_keys"][i].reshape(2 * PEER_HEADS, PEER_NKEYS, PEER_HALF))
    sel_a = np.zeros((STAIR_ROWS, PEER_TOPK), np.float32)
    sel_b = np.zeros((STAIR_ROWS, PEER_TOPK), np.float32)
    for row_i, (a, b) in enumerate(_STAIR):
        sel_a[row_i, a] = 1.0
        sel_b[row_i, b] = 1.0
    return dict(
        norm_mix=_row(w["norm_mix"][i]), norm_ffn=_row(w["norm_ffn"][i]), ple_norm=_row(w["ple_norm"][i]),
        wq_t=w["peer_wq"][i].T.astype(BF16), keys_hi=keys_hi, keys_lo=keys_lo,
        sel_a=jnp.asarray(sel_a, BF16), sel_b=jnp.asarray(sel_b, BF16), sel_at=jnp.asarray(sel_a.T, BF16),
        peer_u=w["peer_u"][i].astype(BF16), peer_vt=w["peer_v"][i].T.astype(BF16),
        w_ple_gate=w["w_ple_gate"][i].astype(BF16), w_ple_proj=w["w_ple_proj"][i].astype(BF16))


def _pad_axis(x, axis, size):
    if x.shape[axis] == size:
        return x
    pads = [(0, 0)] * x.ndim
    pads[axis] = (0, size - x.shape[axis])
    return jnp.pad(x, pads)


def _round_up(n, m):
    return -(-n // m) * m


def _even_layer(h, conv_st, ssm_st, pool_st, pos0, p, cfg):
    b, l, _ = h.shape
    z, xbc, u, dt, dtt = _even_in(h.reshape(b * l, D_MODEL), p["norm_mix"], p["w_in"], p["w_dt_t"], cfg["tm"])
    z, xbc, u, dt = (a.reshape(b, l, -1) for a in (z, xbc, u, dt))
    dtt = dtt.reshape(SSD_HEADS, b, l).transpose(1, 0, 2)
    lp = _round_up(l, SSD_BLOCK)
    r = SSD_HEADS // SSD_GROUPS
    s0 = ssm_st.astype(F32).reshape(b, SSD_GROUPS, r, SSD_HEAD_DIM, SSD_STATE).transpose(0, 1, 4, 2, 3)
    s0 = s0.reshape(b, SSD_GROUPS, SSD_STATE, r * SSD_HEAD_DIM)
    hist = jnp.pad(conv_st.astype(F32), ((0, 0), (8 - (CONV_WIDTH - 1), 0), (0, 0)))
    y, s_fin = _ssd(_pad_axis(xbc, 1, lp), _pad_axis(z, 1, lp), _pad_axis(dt, 1, lp), _pad_axis(dtt, 2, lp),
                    hist, s0, p, l)
    new_ssm = s_fin.reshape(b, SSD_GROUPS, SSD_STATE, r, SSD_HEAD_DIM).transpose(0, 1, 3, 4, 2)
    new_ssm = new_ssm.reshape(b, SSD_HEADS, SSD_HEAD_DIM, SSD_STATE)
    new_conv = jnp.concatenate([conv_st.astype(F32), xbc], axis=1)[:, -(CONV_WIDTH - 1):]
    new_pool = jnp.concatenate([pool_st.astype(F32), u], axis=1)[:, -POOL_HIST:]
    phist = jnp.pad(pool_st.astype(F32), ((0, 0), (16 - POOL_HIST, 0), (0, 0)))
    h_new = _pool_out(u, phist, y[:, :l], h, p, pos0, cfg["tm_seq"])
    return h_new, new_conv, new_ssm, new_pool


def _odd_layer(h, ckv_hist, kpe_hist, pos0, p, cfg):
    b, l, _ = h.shape
    half = QK_ROPE // 2
    pos = (pos0 + jnp.arange(l)).astype(F32)
    inv = ROPE_THETA ** (-jnp.arange(half, dtype=F32) / half)
    ang = pos[:, None] * inv[None, :]
    cos = jnp.tile(jnp.cos(ang), (b, 1))
    sin = jnp.tile(jnp.sin(ang), (b, 1))
    hf = h.reshape(b * l, D_MODEL)
    ckv, kpe, qt = _odd_in(hf, cos, sin, p, cfg["tm"])
    ckv = ckv.reshape(b, l, KV_LORA)
    kpe = kpe.reshape(b, l, QK_ROPE)
    ckv_all = jnp.concatenate([ckv_hist.astype(F32), ckv], axis=1)
    kpe_all = jnp.concatenate([kpe_hist.astype(F32), kpe], axis=1)
    n_keys = ckv_all.shape[1]
    causal = ckv_hist.shape[1] == 0
    if causal:
        tq, tk, nk_pad, lq = cfg["tq"], cfg["tk"], n_keys, l
    else:
        nk_pad = _round_up(n_keys, ATTN_SUB)
        tq, tk, lq = LANES, nk_pad, _round_up(l, LANES)
    k, vt = _kv_expand(_pad_axis(ckv_all, 1, nk_pad), _pad_axis(kpe_all, 1, nk_pad), p, cfg["tm_kv"])
    qt = _pad_axis(qt.reshape(QK_ROWS, b, l).transpose(1, 0, 2), 2, lq)
    ot = _attention(qt, k, vt, tq, tk, causal, None if causal else n_keys)
    ot = ot[:, :, :l].transpose(1, 0, 2).reshape(MLA_HEADS * V_DIM, b * l)
    h_new = _attn_out(ot, hf, p["w_out"], cfg["tm"]).reshape(b, l, D_MODEL)
    return h_new, ckv, kpe


def _prep_layers(w):
    layers = []
    for i in range(DEPTH):
        pc = _prep_common(w, i)
        layers.append(dict(pc, **(_prep_even(w, i // 2) if i % 2 == 0 else _prep_odd(w, i // 2))))
    return layers


def _trunk(x, pemb, conv_st, ssm_st, pool_st, ckv_h, kpe_h, pos0, layers, fnw, cfg):
    b, l, _ = x.shape
    h = x.astype(F32)
    convs, ssms, pools, ckvs, kpes = [], [], [], [], []
    for i in range(DEPTH):
        pc = layers[i]
        if i % 2 == 0:
            e = i // 2
            h, c_new, s_new, p_new = _even_layer(h, conv_st[e], ssm_st[e], pool_st[e], pos0, pc, cfg)
            convs.append(c_new)
            ssms.append(s_new)
            pools.append(p_new)
        else:
            o = i // 2
            h, ckv_new, kpe_new = _odd_layer(h, ckv_h[o], kpe_h[o], pos0, pc, cfg)
            ckvs.append(ckv_new)
            kpes.append(kpe_new)
        hf = h.reshape(b * l, D_MODEL)
        xn, e1, th, e2, kb = _peer_route(hf, pc, cfg["tm_route"])
        hf = _peer_mix(hf, xn, e1, th, e2, kb, pc["peer_u"], pc["peer_vt"], cfg["tm_mix"])
        hf = _ple(hf, pemb[i].reshape(b * l, -1), pc, fnw, i == DEPTH - 1, cfg["tm"])
        h = hf.reshape(b, l, D_MODEL)
    return h, jnp.stack(convs), jnp.stack(ssms), jnp.stack(pools), jnp.stack(ckvs), jnp.stack(kpes)


def _tile(n, pref):
    t = min(pref, n)
    while n % t:
        t -= 8
    return t


def _config(b, l):
    t = b * l
    return dict(tm=_tile(t, 512), tm_seq=_tile(l, 512), tm_route=_tile(t, 512), tm_mix=_tile(t, 1024),
                tq=_tile(l, 1024), tk=_tile(l, 1024), tm_kv=LANES)


def kernel(x_prompt, x_sample, state_conv, state_ssm, state_pool, cache_ckv, cache_kpe, p_prompt, p_sample,
           norm_mix, norm_ffn, ple_norm, final_norm, w_in_e, conv_w, conv_b, dt_bias, a_log, d_skip, ssd_norm_w,
           pool_w, pool_scale, w_out_e, w_in_o, q_norm, kv_norm, w_uq, w_ukv, w_out_o, peer_wq, peer_keys, peer_u,
           peer_v, w_ple_proj, w_ple_gate):
    w = dict(norm_mix=norm_mix, norm_ffn=norm_ffn, ple_norm=ple_norm, final_norm=final_norm, w_in_e=w_in_e,
             conv_w=conv_w, conv_b=conv_b, dt_bias=dt_bias, a_log=a_log, d_skip=d_skip, ssd_norm_w=ssd_norm_w,
             pool_w=pool_w, pool_scale=pool_scale, w_out_e=w_out_e, w_in_o=w_in_o, q_norm=q_norm, kv_norm=kv_norm,
             w_uq=w_uq, w_ukv=w_ukv, w_out_o=w_out_o, peer_wq=peer_wq, peer_keys=peer_keys, peer_u=peer_u,
             peer_v=peer_v, w_ple_proj=w_ple_proj, w_ple_gate=w_ple_gate)
    n_even, n_odd = (DEPTH + 1) // 2, DEPTH // 2
    b0, l0, _ = x_prompt.shape
    conv0 = jnp.zeros((n_even, b0, CONV_WIDTH - 1, CONV_CH), F32)
    ssm0 = jnp.zeros((n_even, b0, SSD_HEADS, SSD_HEAD_DIM, SSD_STATE), F32)
    pool0 = jnp.zeros((n_even, b0, POOL_HIST, SSD_WIDTH), F32)
    ckv0 = jnp.zeros((n_odd, b0, 0, KV_LORA), F32)
    kpe0 = jnp.zeros((n_odd, b0, 0, QK_ROPE), F32)
    layers = _prep_layers(w)
    fnw = _row(final_norm)
    outs_p = _trunk(x_prompt, p_prompt, conv0, ssm0, pool0, ckv0, kpe0, 0, layers, fnw, _config(b0, l0))
    b1, l1, _ = x_sample.shape
    pos0 = cache_ckv.shape[2]
    outs_s = _trunk(x_sample, p_sample, state_conv, state_ssm, state_pool, cache_ckv, cache_kpe, pos0, layers, fnw,
                    _config(b1, l1))
    return (outs_p[0], outs_s[0]) + outs_p[1:] + outs_s[1:]
```
